```python
import math
import jax, jax.numpy as jnp
from jax import lax
import numpy as np

D_MODEL = 2048
BATCH = 8
SEQ = 2048
DEPTH = 2

SSD_EXPAND = 2
D_INNER = SSD_EXPAND * D_MODEL
SSD_HEAD_DIM = 64
SSD_HEADS = D_INNER // SSD_HEAD_DIM
SSD_GROUPS = 8
SSD_HEADS_PER_GROUP = SSD_HEADS // SSD_GROUPS
SSD_STATE = 128
SSD_CONV = 4
SSD_CHUNK = 128
CONV_DIM = D_INNER + 2 * SSD_GROUPS * SSD_STATE
RMS_EPS = 1e-5

ATT_HEAD_DIM = 128
ATT_GROUPS = ((128, 1), (512, 4), (2048, 16))
ATT_HEADS_PER_GROUP = 4
ATT_HEADS = ATT_HEADS_PER_GROUP * len(ATT_GROUPS)
ATT_WIDTH = ATT_HEADS * ATT_HEAD_DIM
ATT_OUT = ATT_HEADS_PER_GROUP * ATT_HEAD_DIM
ATT_BLOCK = 128
ROPE_THETA = 500000.0
ROPE_DIM = ATT_HEAD_DIM // 4

IN_SIZES = (D_INNER, CONV_DIM, SSD_HEADS, ATT_WIDTH, ATT_WIDTH, ATT_WIDTH, D_MODEL, D_MODEL)
IN_DIM = sum(IN_SIZES)

N_EXPERTS = 64
N_EXPERT_GROUPS = 8
TOPK_GROUPS = 4
TOP_K = 8
D_EXPERT = 512
D_SHARED = 512
ROUTED_SCALE = 2.5
MOE_BLOCK = 128

DEEPNORM_ALPHA = (2 * DEPTH) ** 0.25
DEEPNORM_BETA = (8 * DEPTH) ** -0.25
LN_EPS = 1e-5

kernel_name = "hybrid_ssd_dilated_attn_moe_deepnorm"


def layer_norm(x, g, b):
    xf = x.astype(jnp.float32)
    mu = jnp.mean(xf, axis=-1, keepdims=True)
    var = jnp.mean(jnp.square(xf - mu), axis=-1, keepdims=True)
    return ((xf - mu) * lax.rsqrt(var + LN_EPS) * g + b).astype(x.dtype)


def rope_partial(a, cos, sin):
    half = ROPE_DIM // 2
    a1, a2, rest = a[..., :half], a[..., half:ROPE_DIM], a[..., ROPE_DIM:]
    rot = jnp.concatenate([a1 * cos - a2 * sin, a2 * cos + a1 * sin], axis=-1).astype(a.dtype)
    return jnp.concatenate([rot, rest], axis=-1)


def causal_depthwise_conv(x, w, b):
    out = lax.conv_general_dilated(
        x, w[:, None, :].astype(x.dtype), window_strides=(1,), padding=[(SSD_CONV - 1, 0)],
        dimension_numbers=('NWC', 'WIO', 'NWC'), feature_group_count=x.shape[-1])
    return out + b


def ssd_chunked(xs, dt, a, bm, cm):
    bsz, s, _ = xs.shape
    nc, q = s // SSD_CHUNK, SSD_CHUNK
    g, r, p, n = SSD_GROUPS, SSD_HEADS_PER_GROUP, SSD_HEAD_DIM, SSD_STATE
    xh = xs.reshape(bsz, nc, q, g, r, p)
    dth = dt.reshape(bsz, nc, q, g, r)
    bc = bm.reshape(bsz, nc, q, g, n)
    cc = cm.reshape(bsz, nc, q, g, n)
    da = (dth * a.reshape(g, r)).transpose(0, 3, 4, 1, 2)
    a_cs = jnp.cumsum(da, axis=-1)
    causal = jnp.tril(jnp.ones((q, q), dtype=bool))
    seg = a_cs[..., :, None] - a_cs[..., None, :]
    decay_in = jnp.exp(jnp.where(causal, seg, -jnp.inf))
    xdt = xh * dth[..., None]
    cb = jnp.einsum('bclgn,bcsgn->bgcls', cc, bc)
    y_diag = jnp.einsum('bgcls,bgrcls,bcsgrp->bclgrp', cb, decay_in, xdt)
    decay_to_end = jnp.exp(a_cs[..., -1:] - a_cs)
    chunk_states = jnp.einsum('bcsgn,bgrcs,bcsgrp->cbgrpn', bc, decay_to_end, xdt)
    chunk_decay = jnp.exp(a_cs[..., -1]).transpose(3, 0, 1, 2)

    def step(h, inp):
        st, dec = inp
        return dec[..., None, None] * h + st, h

    _, prev_states = lax.scan(step, jnp.zeros_like(chunk_states[0]), (chunk_states, chunk_decay))
    y_off = jnp.einsum('bclgn,cbgrpn,bgrcl->bclgrp', cc, prev_states, jnp.exp(a_cs))
    return (y_diag + y_off).reshape(bsz, s, SSD_HEADS, SSD_HEAD_DIM)


def ssd_branch(xbc, z, dt_raw, conv_w, conv_b, dt_bias, a_log, d_skip, norm_w):
    bsz, s, _ = z.shape
    xbc = jax.nn.silu(causal_depthwise_conv(xbc, conv_w, conv_b))
    xs, bm, cm = jnp.split(xbc, [D_INNER, D_INNER + SSD_GROUPS * SSD_STATE], axis=-1)
    dt = jax.nn.softplus(dt_raw.astype(jnp.float32) + dt_bias.astype(jnp.float32))
    a = -jnp.exp(a_log.astype(jnp.float32))
    y = ssd_chunked(xs, dt, a, bm, cm)
    y = y + d_skip[:, None] * xs.reshape(bsz, s, SSD_HEADS, SSD_HEAD_DIM)
    y = y.reshape(bsz, s, D_INNER) * jax.nn.silu(z)
    yg = y.astype(jnp.float32).reshape(bsz, s, SSD_GROUPS, D_INNER // SSD_GROUPS)
    yg = yg * lax.rsqrt(jnp.mean(jnp.square(yg), axis=-1, keepdims=True) + RMS_EPS)
    return (yg.reshape(bsz, s, D_INNER) * norm_w).astype(z.dtype)


def dilated_window_attention(q, k, v, window, dilation):
    bsz, s, hg, dh = q.shape
    w_sub = window // dilation
    length = s // dilation
    nb = -(-length // ATT_BLOCK)
    lp = nb * ATT_BLOCK

    def to_blocks(a):
        a = a.reshape(bsz, length, dilation, hg, dh).transpose(0, 2, 3, 1, 4)
        a = jnp.pad(a, ((0, 0), (0, 0), (0, 0), (0, lp - length), (0, 0)))
        return a.reshape(bsz, dilation, hg, nb, ATT_BLOCK, dh)

    def with_prev(a):
        prev = jnp.pad(a[:, :, :, :-1], ((0, 0), (0, 0), (0, 0), (1, 0), (0, 0), (0, 0)))
        return jnp.concatenate([prev, a], axis=-2)

    qb = to_blocks(q)
    kc = with_prev(to_blocks(k))
    vc = with_prev(to_blocks(v))
    sc = jnp.einsum('bdhnqe,bdhnke->bdhnqk', qb, kc).astype(jnp.float32) * (dh ** -0.5)
    qi = jnp.arange(ATT_BLOCK)[:, None]
    kj = jnp.arange(2 * ATT_BLOCK)[None, :]
    rel = kj - ATT_BLOCK - qi
    nidx = jnp.arange(nb)[:, None, None]
    mask = (rel <= 0) & (rel >= -w_sub) & ((nidx > 0) | (kj >= ATT_BLOCK))
    sc = jnp.where(mask, sc, -jnp.inf)
    m = jnp.max(sc, axis=-1, keepdims=True)
    pr = jnp.exp(sc - m)
    den = jnp.sum(pr, axis=-1)
    o = jnp.einsum('bdhnqk,bdhnke->bdhnqe', pr, vc.astype(jnp.float32)) / den[..., None]
    lse = m[..., 0] + jnp.log(den)
    o = o.reshape(bsz, dilation, hg, lp, dh)[:, :, :, :length]
    o = o.transpose(0, 3, 1, 2, 4).reshape(bsz, s, hg, dh)
    lse = lse.reshape(bsz, dilation, hg, lp)[..., :length].transpose(0, 3, 1, 2).reshape(bsz, s, hg)
    return o, lse


def attention_branch(q, k, v, cos, sin):
    bsz, s, _ = q.shape
    q = rope_partial(q.reshape(bsz, s, ATT_HEADS, ATT_HEAD_DIM), cos, sin)
    k = rope_partial(k.reshape(bsz, s, ATT_HEADS, ATT_HEAD_DIM), cos, sin)
    v = v.reshape(bsz, s, ATT_HEADS, ATT_HEAD_DIM)
    outs, lses = [], []
    for gi, (window, dilation) in enumerate(ATT_GROUPS):
        sl = slice(gi * ATT_HEADS_PER_GROUP, (gi + 1) * ATT_HEADS_PER_GROUP)
        o, l = dilated_window_attention(q[:, :, sl], k[:, :, sl], v[:, :, sl], window, dilation)
        outs.append(o)
        lses.append(l)
    alpha = jax.nn.softmax(jnp.stack(lses, axis=0), axis=0)
    o = jnp.sum(alpha[..., None] * jnp.stack(outs, axis=0), axis=0)
    return o.reshape(bsz, s, ATT_OUT).astype(q.dtype)


def moe_ffn(x2d, w_router, router_bias, w_gate_e, w_up_e, w_down_e, w_gate_s, w_up_s, w_down_s):
    t, d = x2d.shape
    scores = jax.nn.sigmoid(jnp.dot(x2d, w_router).astype(jnp.float32))
    sel = scores + router_bias.astype(jnp.float32)
    grp = sel.reshape(t, N_EXPERT_GROUPS, N_EXPERTS // N_EXPERT_GROUPS)
    grp_score = jnp.sum(lax.top_k(grp, 2)[0], axis=-1)
    _, top_g = lax.top_k(grp_score, TOPK_GROUPS)
    gmask = jnp.any(top_g[..., None] == jnp.arange(N_EXPERT_GROUPS), axis=-2)
    emask = jnp.repeat(gmask, N_EXPERTS // N_EXPERT_GROUPS, axis=-1)
    _, idx = lax.top_k(jnp.where(emask, sel, -jnp.inf), TOP_K)
    wsel = jnp.take_along_axis(scores, idx, axis=-1)
    wsel = wsel / jnp.sum(wsel, axis=-1, keepdims=True) * ROUTED_SCALE

    tk = t * TOP_K
    flat_e = idx.reshape(-1)
    flat_w = wsel.reshape(-1)
    flat_tok = jnp.arange(tk, dtype=jnp.int32) // TOP_K
    order = jnp.argsort(flat_e)
    se = flat_e[order]
    counts = jnp.bincount(flat_e, length=N_EXPERTS)
    padded = (counts + MOE_BLOCK - 1) // MOE_BLOCK * MOE_BLOCK
    pad_end = jnp.cumsum(padded)
    pad_start = pad_end - padded
    start = jnp.cumsum(counts) - counts
    dest = pad_start[se] + jnp.arange(tk) - start[se]
    n_blocks = -(-tk // MOE_BLOCK) + N_EXPERTS
    n_rows = n_blocks * MOE_BLOCK
    row_tok = jnp.full((n_rows,), t, dtype=jnp.int32).at[dest].set(flat_tok[order])
    row_w = jnp.zeros((n_rows,), flat_w.dtype).at[dest].set(flat_w[order])
    block_e = jnp.minimum(jnp.searchsorted(pad_end, jnp.arange(n_blocks) * MOE_BLOCK, side='right'),
                          N_EXPERTS - 1)
    x_pad = jnp.concatenate([x2d, jnp.zeros((1, d), x2d.dtype)], axis=0)

    def expert_block(args):
        tok, wt, e = args
        xb = x_pad[tok]
        h = jax.nn.silu(xb @ w_gate_e[e]) * (xb @ w_up_e[e])
        return (h @ w_down_e[e]) * wt[:, None]

    y_rows = lax.map(expert_block, (row_tok.reshape(n_blocks, MOE_BLOCK),
                                    row_w.reshape(n_blocks, MOE_BLOCK), block_e))
    routed = jax.ops.segment_sum(y_rows.reshape(n_rows, d), row_tok, num_segments=t + 1)[:t]
    shared = (jax.nn.silu(x2d @ w_gate_s) * (x2d @ w_up_s)) @ w_down_s
    return (shared + routed).astype(x2d.dtype)


def hybrid_layer(x, cos, sin, w_in, conv_w, conv_b, dt_bias, a_log, d_skip, ssd_norm_w,
                 w_ssd_br, w_att_br, gate_b, w_o, ln1_g, ln1_b, w_router, router_bias,
                 w_gate_e, w_up_e, w_down_e, w_gate_s, w_up_s, w_down_s, ln2_g, ln2_b):
    bsz, s, d = x.shape
    splits = np.cumsum(IN_SIZES)[:-1].tolist()
    u = x @ w_in
    z, xbc, dt_raw, q, k, v, g_ssd, g_att = jnp.split(u, splits, axis=-1)
    y_ssd = ssd_branch(xbc, z, dt_raw, conv_w, conv_b, dt_bias, a_log, d_skip, ssd_norm_w) @ w_ssd_br
    y_att = attention_branch(q, k, v, cos, sin) @ w_att_br
    merged = jax.nn.sigmoid(g_ssd + gate_b[0]) * y_ssd + jax.nn.sigmoid(g_att + gate_b[1]) * y_att
    x = layer_norm(DEEPNORM_ALPHA * x + merged @ w_o, ln1_g, ln1_b)
    ffn = moe_ffn(x.reshape(bsz * s, d), w_router, router_bias, w_gate_e, w_up_e, w_down_e,
                  w_gate_s, w_up_s, w_down_s).reshape(bsz, s, d)
    return layer_norm(DEEPNORM_ALPHA * x + ffn, ln2_g, ln2_b)


def setup_inputs(seed: int = 0) -> dict:
    key = jax.random.key(seed)
    ks = jax.random.split(key, 26)
    f32 = jnp.float32
    L = DEPTH

    def nrm(k, shape, scale):
        return jax.random.normal(k, shape, f32) * scale

    x = nrm(ks[0], (BATCH, SEQ, D_MODEL), 1.0)
    positions = (jax.random.randint(ks[1], (BATCH, 1), 0, 4096, dtype=jnp.int32)
                 + jnp.arange(SEQ, dtype=jnp.int32)[None, :])
    w_in = nrm(ks[2], (L, D_MODEL, IN_DIM), D_MODEL ** -0.5)
    conv_w = nrm(ks[3], (L, SSD_CONV, CONV_DIM), SSD_CONV ** -0.5)
    conv_b = nrm(ks[4], (L, CONV_DIM), 0.02)
    dt0 = jnp.exp(jax.random.uniform(ks[5], (L, SSD_HEADS), f32, math.log(1e-3), math.log(1e-1)))
    dt_bias = dt0 + jnp.log(-jnp.expm1(-dt0))
    a_log = jnp.log(jax.random.uniform(ks[6], (L, SSD_HEADS), f32, 1.0, 16.0))
    d_skip = 1.0 + nrm(ks[7], (L, SSD_HEADS), 0.1)
    ssd_norm_w = 1.0 + nrm(ks[8], (L, D_INNER), 0.02)
    w_ssd_br = nrm(ks[9], (L, D_INNER, D_MODEL), D_INNER ** -0.5)
    w_att_br = nrm(ks[10], (L, ATT_OUT, D_MODEL), ATT_OUT ** -0.5)
    gate_b = nrm(ks[11], (L, 2, D_MODEL), 0.02)
    w_o = nrm(ks[12], (L, D_MODEL, D_MODEL), D_MODEL ** -0.5 * DEEPNORM_BETA)
    ln1_g = 1.0 + nrm(ks[13], (L, D_MODEL), 0.02)
    ln1_b = nrm(ks[14], (L, D_MODEL), 0.02)
    w_router = nrm(ks[15], (L, D_MODEL, N_EXPERTS), D_MODEL ** -0.5)
    router_bias = nrm(ks[16], (L, N_EXPERTS), 0.01)
    w_gate_e = nrm(ks[17], (L, N_EXPERTS, D_MODEL, D_EXPERT), D_MODEL ** -0.5)
    w_up_e = nrm(ks[18], (L, N_EXPERTS, D_MODEL, D_EXPERT), D_MODEL ** -0.5)
    w_down_e = nrm(ks[19], (L, N_EXPERTS, D_EXPERT, D_MODEL), D_EXPERT ** -0.5 * DEEPNORM_BETA)
    w_gate_s = nrm(ks[20], (L, D_MODEL, D_SHARED), D_MODEL ** -0.5)
    w_up_s = nrm(ks[21], (L, D_MODEL, D_SHARED), D_MODEL ** -0.5)
    w_down_s = nrm(ks[22], (L, D_SHARED, D_MODEL), D_SHARED ** -0.5 * DEEPNORM_BETA)
    ln2_g = 1.0 + nrm(ks[23], (L, D_MODEL), 0.02)
    ln2_b = nrm(ks[24], (L, D_MODEL), 0.02)
    return {"x": x, "positions": positions, "w_in": w_in, "conv_w": conv_w, "conv_b": conv_b,
            "dt_bias": dt_bias, "a_log": a_log, "d_skip": d_skip, "ssd_norm_w": ssd_norm_w,
            "w_ssd_br": w_ssd_br, "w_att_br": w_att_br, "gate_b": gate_b, "w_o": w_o,
            "ln1_g": ln1_g, "ln1_b": ln1_b, "w_router": w_router, "router_bias": router_bias,
            "w_gate_e": w_gate_e, "w_up_e": w_up_e, "w_down_e": w_down_e,
            "w_gate_s": w_gate_s, "w_up_s": w_up_s, "w_down_s": w_down_s,
            "ln2_g": ln2_g, "ln2_b": ln2_b}


def reference(x, positions, w_in, conv_w, conv_b, dt_bias, a_log, d_skip, ssd_norm_w,
              w_ssd_br, w_att_br, gate_b, w_o, ln1_g, ln1_b, w_router, router_bias,
              w_gate_e, w_up_e, w_down_e, w_gate_s, w_up_s, w_down_s, ln2_g, ln2_b):
    inv_freq = ROPE_THETA ** (-jnp.arange(0, ROPE_DIM, 2, dtype=jnp.float32) / ROPE_DIM)
    ang = positions.astype(jnp.float32)[..., None] * inv_freq
    cos = jnp.cos(ang)[:, :, None, :]
    sin = jnp.sin(ang)[:, :, None, :]
    for l in range(DEPTH):
        x = hybrid_layer(x, cos, sin, w_in[l], conv_w[l], conv_b[l], dt_bias[l], a_log[l],
                         d_skip[l], ssd_norm_w[l], w_ssd_br[l], w_att_br[l], gate_b[l], w_o[l],
                         ln1_g[l], ln1_b[l], w_router[l], router_bias[l], w_gate_e[l], w_up_e[l],
                         w_down_e[l], w_gate_s[l], w_up_s[l], w_down_s[l], ln2_g[l], ln2_b[l])
    return x
```

```python
import functools
import math

import jax
import jax.numpy as jnp
from jax import lax
from jax.experimental import pallas as pl
from jax.experimental.pallas import tpu as pltpu

D_MODEL = 2048
DEPTH = 2

D_INNER = 4096
SSD_HEAD_DIM = 64
SSD_HEADS = 64
SSD_GROUPS = 8
SSD_HEADS_PER_GROUP = 8
SSD_STATE = 128
SSD_CONV = 4
SSD_CHUNK = 128
CONV_DIM = D_INNER + 2 * SSD_GROUPS * SSD_STATE
RMS_EPS = 1e-5

ATT_HEAD_DIM = 128
ATT_GROUPS = ((128, 1), (512, 4), (2048, 16))
ATT_HEADS_PER_GROUP = 4
ATT_WIDTH = 1536
ATT_OUT = 512
ATT_BLOCK = 128
ROPE_THETA = 500000.0
ROPE_DIM = 32

IN_SIZES = (D_INNER, CONV_DIM, SSD_HEADS, ATT_WIDTH, ATT_WIDTH, ATT_WIDTH, D_MODEL, D_MODEL)

N_EXPERTS = 64
N_EXPERT_GROUPS = 8
EXPERTS_PER_GROUP = 8
TOPK_GROUPS = 4
TOP_K = 8
D_EXPERT = 512
D_SHARED = 512
ROUTED_SCALE = 2.5

DEEPNORM_ALPHA = (2 * DEPTH) ** 0.25
LN_EPS = 1e-5

LANES = 128
HEAD_PAD = 128
MOE_ROWS = 256
COMBINE_TOKENS = 128
VMEM_LIMIT = 56 * 1024 * 1024

F32 = jnp.float32
BF16 = jnp.bfloat16


def _cparams(*sem):
    return pltpu.CompilerParams(dimension_semantics=sem, vmem_limit_bytes=VMEM_LIMIT)


def _sigmoid(v):
    return 1.0 / (1.0 + jnp.exp(-v))


def _mm_kernel(x_ref, w_ref, b_ref, o_ref, *, act):
    acc = jnp.dot(x_ref[...], w_ref[...], preferred_element_type=F32) + b_ref[...]
    if act == "silu":
        acc = acc * _sigmoid(acc)
    elif act == "sigmoid":
        acc = _sigmoid(acc)
    elif act == "softplus":
        acc = jnp.maximum(acc, 0.0) + jnp.log(1.0 + jnp.exp(-jnp.abs(acc)))
    o_ref[...] = acc.astype(o_ref.dtype)


def _matmul(x, w, bias, act, out_dtype, tm, tn):
    m, k = x.shape
    n = w.shape[1]
    return pl.pallas_call(
        functools.partial(_mm_kernel, act=act),
        out_shape=jax.ShapeDtypeStruct((m, n), out_dtype),
        grid=(m // tm, n // tn),
        in_specs=[pl.BlockSpec((tm, k), lambda i, j: (i, 0)),
                  pl.BlockSpec((k, tn), lambda i, j: (0, j)),
                  pl.BlockSpec((1, tn), lambda i, j: (0, j))],
        out_specs=pl.BlockSpec((tm, tn), lambda i, j: (i, j)),
        compiler_params=_cparams("parallel", "arbitrary"),
        name="proj_" + act,
    )(x, w, bias)


def _split_dot(v, e):
    hi = v.astype(BF16)
    lo = (v - hi.astype(F32)).astype(BF16)
    return (jnp.dot(hi, e, preferred_element_type=F32)
            + jnp.dot(lo, e, preferred_element_type=F32))


def _ssd_kernel(xbc_ref, zs_ref, dt_ref, convw_ref, convb_ref, alog_ref, dskip_ref, normw_ref,
                expand_ref, o_ref, xpad_ref, state_ref, y_ref, dtx_ref, dtex_ref, eax_ref):
    q = SSD_CHUNK
    gw = SSD_HEADS_PER_GROUP * SSD_HEAD_DIM

    @pl.when(pl.program_id(1) == 0)
    def _():
        xpad_ref[0:8, :] = jnp.zeros((8, CONV_DIM), F32)
        state_ref[...] = jnp.zeros_like(state_ref)

    xpad_ref[8:8 + q, :] = xbc_ref[0].astype(F32)

    def conv(c0, width):
        acc = convb_ref[:, c0:c0 + width]
        for j in range(SSD_CONV):
            lo = 8 - (SSD_CONV - 1) + j
            acc = acc + convw_ref[j:j + 1, c0:c0 + width] * xpad_ref[lo:lo + q, c0:c0 + width]
        return acc * _sigmoid(acc)

    row = lax.broadcasted_iota(jnp.int32, (q, q), 0)
    col = lax.broadcasted_iota(jnp.int32, (q, q), 1)
    causal = col <= row
    tril = jnp.where(causal, 1.0, 0.0).astype(BF16)

    dt = dt_ref[0]
    a = -jnp.exp(alog_ref[...])
    da = dt * a
    a_cs = _split_dot_left(tril, da)
    a_cs_t = a_cs.T
    a_last = a_cs[q - 1:q, :]
    expand = expand_ref[...]
    dtx_ref[...] = _split_dot(dt, expand)
    dtex_ref[...] = _split_dot(dt * jnp.exp(a_last - a_cs), expand)
    eax_ref[...] = _split_dot(jnp.exp(a_cs), expand)
    decx = _split_dot(jnp.broadcast_to(jnp.exp(a_last), (8, HEAD_PAD)), expand)[0:1, :]
    dskipx = _split_dot(jnp.broadcast_to(dskip_ref[...], (8, HEAD_PAD)), expand)[0:1, :]

    for g in range(SSD_GROUPS):
        c0 = g * gw
        xs = conv(c0, gw)
        bm = conv(D_INNER + g * SSD_STATE, SSD_STATE)
        cm = conv(D_INNER + SSD_GROUPS * SSD_STATE + g * SSD_STATE, SSD_STATE)
        bm_t = bm.T.astype(BF16)
        cm16 = cm.astype(BF16)
        cb = jnp.dot(cm16, bm_t, preferred_element_type=F32)
        xdt = (xs * dtx_ref[:, c0:c0 + gw]).astype(BF16)
        for r in range(SSD_HEADS_PER_GROUP):
            h = g * SSD_HEADS_PER_GROUP + r
            seg = a_cs[:, h:h + 1] - a_cs_t[h:h + 1, :]
            decay = jnp.exp(jnp.where(causal, seg, -jnp.inf))
            m = (cb * decay).astype(BF16)
            y_ref[:, h * SSD_HEAD_DIM:(h + 1) * SSD_HEAD_DIM] = jnp.dot(
                m, xdt[:, r * SSD_HEAD_DIM:(r + 1) * SSD_HEAD_DIM], preferred_element_type=F32)
        st = state_ref[:, c0:c0 + gw]
        y_off = jnp.dot(cm16, st.astype(BF16), preferred_element_type=F32) * eax_ref[:, c0:c0 + gw]
        xdte = (xs * dtex_ref[:, c0:c0 + gw]).astype(BF16)
        state_ref[:, c0:c0 + gw] = decx[:, c0:c0 + gw] * st + jnp.dot(
            bm_t, xdte, preferred_element_type=F32)
        y = y_ref[:, c0:c0 + gw] + y_off + dskipx[:, c0:c0 + gw] * xs
        y = y * zs_ref[0, :, c0:c0 + gw].astype(F32)
        ms = jnp.mean(y * y, axis=-1, keepdims=True)
        o_ref[0, :, c0:c0 + gw] = (y * lax.rsqrt(ms + RMS_EPS)
                                   * normw_ref[:, c0:c0 + gw]).astype(o_ref.dtype)

    xpad_ref[0:8, :] = xpad_ref[q:q + 8, :]


def _split_dot_left(e, v):
    hi = v.astype(BF16)
    lo = (v - hi.astype(F32)).astype(BF16)
    return (jnp.dot(e, hi, preferred_element_type=F32)
            + jnp.dot(e, lo, preferred_element_type=F32))


def _ssd(xbc, zs, dt, conv_w, conv_b, a_log, d_skip, norm_w, expand):
    bsz, s, _ = xbc.shape
    nc = s // SSD_CHUNK
    full = lambda shape: pl.BlockSpec(shape, lambda b, c: (0, 0))
    return pl.pallas_call(
        _ssd_kernel,
        out_shape=jax.ShapeDtypeStruct((bsz, s, D_INNER), BF16),
        grid=(bsz, nc),
        in_specs=[pl.BlockSpec((1, SSD_CHUNK, CONV_DIM), lambda b, c: (b, c, 0)),
                  pl.BlockSpec((1, SSD_CHUNK, D_INNER), lambda b, c: (b, c, 0)),
                  pl.BlockSpec((1, SSD_CHUNK, HEAD_PAD), lambda b, c: (b, c, 0)),
                  full((SSD_CONV, CONV_DIM)), full((1, CONV_DIM)), full((1, HEAD_PAD)),
                  full((1, HEAD_PAD)), full((1, D_INNER)), full((HEAD_PAD, D_INNER))],
        out_specs=pl.BlockSpec((1, SSD_CHUNK, D_INNER), lambda b, c: (b, c, 0)),
        scratch_shapes=[pltpu.VMEM((SSD_CHUNK + 8, CONV_DIM), F32),
                        pltpu.VMEM((SSD_STATE, D_INNER), F32),
                        pltpu.VMEM((SSD_CHUNK, D_INNER), F32),
                        pltpu.VMEM((SSD_CHUNK, D_INNER), F32),
                        pltpu.VMEM((SSD_CHUNK, D_INNER), F32),
                        pltpu.VMEM((SSD_CHUNK, D_INNER), F32)],
        compiler_params=_cparams("parallel", "arbitrary"),
        name="ssd_mixer",
    )(xbc, zs, dt, conv_w, conv_b, a_log, d_skip, norm_w, expand)


def _attn_kernel(q_ref, kc_ref, kp_ref, vc_ref, vp_ref, cc_ref, sc_ref, cp_ref, sp_ref, perm_ref,
                 o_ref, lse_ref):
    blk, dh = ATT_BLOCK, ATT_HEAD_DIM
    has_prev = pl.program_id(2) > 0
    qi = lax.broadcasted_iota(jnp.int32, (blk, blk), 0)
    kj = lax.broadcasted_iota(jnp.int32, (blk, blk), 1)
    mask_c = kj <= qi
    mask_p = kj >= qi
    prev_bias = jnp.where(has_prev, 0.0, -jnp.inf)
    perm = perm_ref[...]
    cos_c, sin_c = cc_ref[0], sc_ref[0]
    cos_p, sin_p = cp_ref[0], sp_ref[0]

    def rope(a, cos, sin):
        swapped = jnp.dot(a, perm, preferred_element_type=F32)
        return (a.astype(F32) * cos + swapped * sin).astype(BF16)

    nt = (((1,), (1,)), ((), ()))
    for h in range(ATT_HEADS_PER_GROUP):
        sl = slice(h * dh, (h + 1) * dh)
        qh = rope(q_ref[0, :, sl], cos_c, sin_c)
        kc = rope(kc_ref[0, :, sl], cos_c, sin_c)
        kp = rope(kp_ref[0, :, sl], cos_p, sin_p)
        s_c = lax.dot_general(qh, kc, nt, preferred_element_type=F32) * (dh ** -0.5)
        s_p = lax.dot_general(qh, kp, nt, preferred_element_type=F32) * (dh ** -0.5)
        s_c = jnp.where(mask_c, s_c, -jnp.inf)
        s_p = jnp.where(mask_p, s_p + prev_bias, -jnp.inf)
        m = jnp.maximum(jnp.max(s_c, axis=-1, keepdims=True), jnp.max(s_p, axis=-1, keepdims=True))
        p_c = jnp.exp(s_c - m)
        p_p = jnp.exp(s_p - m)
        den = jnp.sum(p_c, axis=-1, keepdims=True) + jnp.sum(p_p, axis=-1, keepdims=True)
        o = (jnp.dot(p_c.astype(BF16), vc_ref[0, :, sl], preferred_element_type=F32)
             + jnp.dot(p_p.astype(BF16), vp_ref[0, :, sl], preferred_element_type=F32)) / den
        o_ref[0, :, sl] = o.astype(o_ref.dtype)
        lse_ref[0, :, sl] = jnp.broadcast_to(m + jnp.log(den), (blk, dh))


def _attention_group(qkv, cosf, sinf, perm, gi, dilation):
    bsz, s, width = qkv.shape
    length = s // dilation
    nb = length // ATT_BLOCK
    gw = ATT_HEADS_PER_GROUP * ATT_HEAD_DIM
    per_row = width // gw
    qkv_v = qkv.reshape(bsz, length, dilation * width)
    cos_v = cosf.reshape(bsz, length, dilation * LANES)
    sin_v = sinf.reshape(bsz, length, dilation * LANES)

    def qkv_spec(which, prev):
        def index_map(b, r, n):
            return (b, jnp.maximum(n - 1, 0) if prev else n, r * per_row + which * 3 + gi)
        return pl.BlockSpec((1, ATT_BLOCK, gw), index_map)

    def tab_spec(prev):
        def index_map(b, r, n):
            return (b, jnp.maximum(n - 1, 0) if prev else n, r)
        return pl.BlockSpec((1, ATT_BLOCK, LANES), index_map)

    out_spec = pl.BlockSpec((1, ATT_BLOCK, gw), lambda b, r, n: (b, n, r))
    o, lse = pl.pallas_call(
        _attn_kernel,
        out_shape=(jax.ShapeDtypeStruct((bsz, length, dilation * gw), BF16),
                   jax.ShapeDtypeStruct((bsz, length, dilation * gw), F32)),
        grid=(bsz, dilation, nb),
        in_specs=[qkv_spec(0, False), qkv_spec(1, False), qkv_spec(1, True),
                  qkv_spec(2, False), qkv_spec(2, True),
                  tab_spec(False), tab_spec(False), tab_spec(True), tab_spec(True),
                  pl.BlockSpec((LANES, LANES), lambda b, r, n: (0, 0))],
        out_specs=(out_spec, out_spec),
        compiler_params=_cparams("parallel", "parallel", "arbitrary"),
        name="dilated_attention_%d" % dilation,
    )(qkv_v, qkv_v, qkv_v, qkv_v, qkv_v, cos_v, sin_v, cos_v, sin_v, perm)
    return o.reshape(bsz * s, gw), lse.reshape(bsz * s, gw)


def _merge_kernel(y_ref, o1_ref, o2_ref, o3_ref, l1_ref, l2_ref, l3_ref, ws_ref, wa_ref,
                  gs_ref, ga_ref, out_ref, att_ref):
    @pl.when(pl.program_id(1) == 0)
    def _():
        l1, l2, l3 = l1_ref[...], l2_ref[...], l3_ref[...]
        m = jnp.maximum(jnp.maximum(l1, l2), l3)
        e1, e2, e3 = jnp.exp(l1 - m), jnp.exp(l2 - m), jnp.exp(l3 - m)
        num = (e1 * o1_ref[...].astype(F32) + e2 * o2_ref[...].astype(F32)
               + e3 * o3_ref[...].astype(F32))
        att_ref[...] = (num / (e1 + e2 + e3)).astype(BF16)

    y_ssd = jnp.dot(y_ref[...], ws_ref[...], preferred_element_type=F32)
    y_att = jnp.dot(att_ref[...], wa_ref[...], preferred_element_type=F32)
    out_ref[...] = (gs_ref[...].astype(F32) * y_ssd
                    + ga_ref[...].astype(F32) * y_att).astype(out_ref.dtype)


def _merge(y_ssd, outs, lses, w_ssd_br, w_att_br, gates, tm=512, tn=512):
    t = y_ssd.shape[0]
    nj = D_MODEL // tn
    row = lambda width: pl.BlockSpec((tm, width), lambda i, j: (i, 0))
    return pl.pallas_call(
        _merge_kernel,
        out_shape=jax.ShapeDtypeStruct((t, D_MODEL), BF16),
        grid=(t // tm, nj),
        in_specs=[row(D_INNER)] + [row(ATT_OUT)] * 6
                 + [pl.BlockSpec((D_INNER, tn), lambda i, j: (0, j)),
                    pl.BlockSpec((ATT_OUT, tn), lambda i, j: (0, j)),
                    pl.BlockSpec((tm, tn), lambda i, j: (i, j)),
                    pl.BlockSpec((tm, tn), lambda i, j: (i, j + nj))],
        out_specs=pl.BlockSpec((tm, tn), lambda i, j: (i, j)),
        scratch_shapes=[pltpu.VMEM((tm, ATT_OUT), BF16)],
        compiler_params=_cparams("parallel", "arbitrary"),
        name="branch_merge",
    )(y_ssd, *outs, *lses, w_ssd_br, w_att_br, gates, gates)


def _layer_norm(v, g, b):
    mu = jnp.mean(v, axis=-1, keepdims=True)
    c = v - mu
    var = jnp.mean(c * c, axis=-1, keepdims=True)
    return c * lax.rsqrt(var + LN_EPS) * g + b


def _proj_ln_kernel(m_ref, w_ref, x_ref, g_ref, b_ref, o_ref, o16_ref):
    v = DEEPNORM_ALPHA * x_ref[...] + jnp.dot(m_ref[...], w_ref[...], preferred_element_type=F32)
    out = _layer_norm(v, g_ref[...], b_ref[...])
    o_ref[...] = out
    o16_ref[...] = out.astype(BF16)


def _proj_ln(merged, w_o, x, g, b, tm=256):
    t = x.shape[0]
    row = lambda: pl.BlockSpec((tm, D_MODEL), lambda i: (i, 0))
    const = lambda shape: pl.BlockSpec(shape, lambda i: (0, 0))
    return pl.pallas_call(
        _proj_ln_kernel,
        out_shape=(jax.ShapeDtypeStruct((t, D_MODEL), F32), jax.ShapeDtypeStruct((t, D_MODEL), BF16)),
        grid=(t // tm,),
        in_specs=[row(), const((D_MODEL, D_MODEL)), row(), const((1, D_MODEL)), const((1, D_MODEL))],
        out_specs=(row(), row()),
        compiler_params=_cparams("parallel"),
        name="out_proj_layernorm",
    )(merged, w_o, x, g, b)


def _router_kernel(x_ref, w_ref, bias_ref, idx_ref, wt_ref):
    tm = x_ref.shape[0]
    ne, ng, eg = N_EXPERTS, N_EXPERT_GROUPS, EXPERTS_PER_GROUP
    logits = lax.dot_general(w_ref[...], x_ref[...], (((1,), (1,)), ((), ())),
                             precision=lax.Precision.HIGHEST, preferred_element_type=F32)
    scores = _sigmoid(logits)
    sel = scores + bias_ref[...]
    sub = lax.broadcasted_iota(jnp.int32, (eg, tm), 0)
    neg = -jnp.inf

    gscore = []
    for g in range(ng):
        sg = sel[g * eg:(g + 1) * eg, :]
        m1 = jnp.max(sg, axis=0, keepdims=True)
        first = jnp.min(jnp.where(sg == m1, sub, eg), axis=0, keepdims=True)
        m2 = jnp.max(jnp.where(sub == first, neg, sg), axis=0, keepdims=True)
        gscore.append(m1 + m2)
    keep = []
    for g in range(ng):
        beaten = jnp.zeros((1, tm), jnp.int32)
        for o in range(ng):
            if o == g:
                continue
            wins = (gscore[o] >= gscore[g]) if o < g else (gscore[o] > gscore[g])
            beaten = beaten + wins.astype(jnp.int32)
        keep.append(beaten < TOPK_GROUPS)
    masked = jnp.concatenate(
        [jnp.where(keep[g], sel[g * eg:(g + 1) * eg, :], neg) for g in range(ng)], axis=0)

    eidx = lax.broadcasted_iota(jnp.int32, (ne, tm), 0)
    picks, weights = [], []
    for _ in range(TOP_K):
        mx = jnp.max(masked, axis=0, keepdims=True)
        pick = jnp.min(jnp.where(masked == mx, eidx, ne), axis=0, keepdims=True)
        chosen = eidx == pick
        weights.append(jnp.sum(jnp.where(chosen, scores, 0.0), axis=0, keepdims=True))
        picks.append(pick)
        masked = jnp.where(chosen, neg, masked)
    w = jnp.concatenate(weights, axis=0)
    idx_ref[...] = jnp.concatenate(picks, axis=0)
    wt_ref[...] = w / jnp.sum(w, axis=0, keepdims=True) * ROUTED_SCALE


def _router(x, w_router_t, bias, tm=512):
    t = x.shape[0]
    return pl.pallas_call(
        _router_kernel,
        out_shape=(jax.ShapeDtypeStruct((TOP_K, t), jnp.int32),
                   jax.ShapeDtypeStruct((TOP_K, t), F32)),
        grid=(t // tm,),
        in_specs=[pl.BlockSpec((tm, D_MODEL), lambda i: (i, 0)),
                  pl.BlockSpec((N_EXPERTS, D_MODEL), lambda i: (0, 0)),
                  pl.BlockSpec((N_EXPERTS, 1), lambda i: (0, 0))],
        out_specs=(pl.BlockSpec((TOP_K, tm), lambda i: (0, i)),
                   pl.BlockSpec((TOP_K, tm), lambda i: (0, i))),
        compiler_params=_cparams("parallel"),
        name="router_topk",
    )(x, w_router_t, bias)


def _gather_rows(src_hbm, idx_ref, dst_ref, sem, n_rows):
    def body(r, carry):
        pltpu.make_async_copy(src_hbm.at[pl.ds(idx_ref[0, 0, r], 1), :],
                              dst_ref.at[pl.ds(r, 1), :], sem).start()
        return carry
    lax.fori_loop(0, n_rows, body, 0, unroll=8)


def _experts_kernel(be_ref, na_ref, tok0_ref, tokn_ref, x_hbm, wrow_ref, wg_ref, wu_ref, wd_ref,
                    y_ref, buf_ref, sem_ref):
    i = pl.program_id(0)
    n_active = na_ref[0]
    slot = lax.rem(i, 2)

    @pl.when(i == 0)
    def _():
        _gather_rows(x_hbm, tok0_ref, buf_ref.at[0], sem_ref.at[0], MOE_ROWS)

    @pl.when(i + 1 < n_active)
    def _():
        _gather_rows(x_hbm, tokn_ref, buf_ref.at[1 - slot], sem_ref.at[1 - slot], MOE_ROWS)

    @pl.when(i < n_active)
    def _():
        pltpu.make_async_copy(x_hbm.at[pl.ds(0, MOE_ROWS), :], buf_ref.at[slot],
                              sem_ref.at[slot]).wait()
        xb = buf_ref[slot].astype(BF16)
        gate = jnp.dot(xb, wg_ref[0], preferred_element_type=F32)
        up = jnp.dot(xb, wu_ref[0], preferred_element_type=F32)
        h = gate * _sigmoid(gate) * up * wrow_ref[...]
        y_ref[...] = jnp.dot(h.astype(BF16), wd_ref[0], preferred_element_type=F32)

    @pl.when(i >= n_active)
    def _():
        y_ref[...] = jnp.zeros_like(y_ref)


def _experts(x, block_e, n_active, row_tok, row_w, w_gate, w_up, w_down):
    n_blocks = row_tok.shape[0]
    n_rows = n_blocks * MOE_ROWS
    smem_blk = lambda index_map: pl.BlockSpec((1, 1, MOE_ROWS), index_map, memory_space=pltpu.SMEM)
    grid_spec = pltpu.PrefetchScalarGridSpec(
        num_scalar_prefetch=2,
        grid=(n_blocks,),
        in_specs=[smem_blk(lambda i, be, na: (0, 0, 0)),
                  smem_blk(lambda i, be, na: (jnp.minimum(i + 1, n_blocks - 1), 0, 0)),
                  pl.BlockSpec(memory_space=pl.ANY),
                  pl.BlockSpec((MOE_ROWS, 1), lambda i, be, na: (i, 0)),
                  pl.BlockSpec((1, D_MODEL, D_EXPERT), lambda i, be, na: (be[i], 0, 0)),
                  pl.BlockSpec((1, D_MODEL, D_EXPERT), lambda i, be, na: (be[i], 0, 0)),
                  pl.BlockSpec((1, D_EXPERT, D_MODEL), lambda i, be, na: (be[i], 0, 0))],
        out_specs=pl.BlockSpec((MOE_ROWS, D_MODEL), lambda i, be, na: (i, 0)),
        scratch_shapes=[pltpu.VMEM((2, MOE_ROWS, D_MODEL), F32),
                        pltpu.SemaphoreType.DMA((2,))])
    return pl.pallas_call(
        _experts_kernel,
        out_shape=jax.ShapeDtypeStruct((n_rows, D_MODEL), F32),
        grid_spec=grid_spec,
        compiler_params=_cparams("arbitrary"),
        name="routed_experts",
    )(block_e, n_active, row_tok, row_tok, x, row_w, w_gate, w_up, w_down)


def _combine_kernel(d0_ref, dn_ref, y_hbm, x_ref, x16_ref, wg_ref, wu_ref, wd_ref, g_ref, b_ref,
                    o_ref, o16_ref, buf_ref, sem_ref):
    i = pl.program_id(0)
    n = pl.num_programs(0)
    slot = lax.rem(i, 2)
    rows = COMBINE_TOKENS * TOP_K

    @pl.when(i == 0)
    def _():
        _gather_rows(y_hbm, d0_ref, buf_ref.at[0], sem_ref.at[0], rows)

    @pl.when(i + 1 < n)
    def _():
        _gather_rows(y_hbm, dn_ref, buf_ref.at[1 - slot], sem_ref.at[1 - slot], rows)

    x16 = x16_ref[...]
    gate = jnp.dot(x16, wg_ref[...], preferred_element_type=F32)
    up = jnp.dot(x16, wu_ref[...], preferred_element_type=F32)
    h = (gate * _sigmoid(gate) * up).astype(BF16)
    v = DEEPNORM_ALPHA * x_ref[...] + jnp.dot(h, wd_ref[...], preferred_element_type=F32)

    pltpu.make_async_copy(y_hbm.at[pl.ds(0, rows), :], buf_ref.at[slot], sem_ref.at[slot]).wait()
    for k in range(TOP_K):
        v = v + buf_ref[slot, k * COMBINE_TOKENS:(k + 1) * COMBINE_TOKENS, :]
    out = _layer_norm(v, g_ref[...], b_ref[...])
    o_ref[...] = out
    o16_ref[...] = out.astype(BF16)


def _combine(y_rows, dest_blocks, x, x16, w_gate_s, w_up_s, w_down_s, g, b):
    t = x.shape[0]
    tm = COMBINE_TOKENS
    rows = tm * TOP_K
    row = lambda: pl.BlockSpec((tm, D_MODEL), lambda i: (i, 0))
    const = lambda shape: pl.BlockSpec(shape, lambda i: (0, 0))
    nblk = t // tm
    smem_blk = lambda index_map: pl.BlockSpec((1, 1, rows), index_map, memory_space=pltpu.SMEM)
    return pl.pallas_call(
        _combine_kernel,
        out_shape=(jax.ShapeDtypeStruct((t, D_MODEL), F32), jax.ShapeDtypeStruct((t, D_MODEL), BF16)),
        grid=(nblk,),
        in_specs=[smem_blk(lambda i: (0, 0, 0)),
                  smem_blk(lambda i: (jnp.minimum(i + 1, nblk - 1), 0, 0)),
                  pl.BlockSpec(memory_space=pl.ANY),
                  row(), row(),
                  const((D_MODEL, D_SHARED)), const((D_MODEL, D_SHARED)), const((D_SHARED, D_MODEL)),
                  const((1, D_MODEL)), const((1, D_MODEL))],
        out_specs=(row(), row()),
        scratch_shapes=[pltpu.VMEM((2, rows, D_MODEL), F32), pltpu.SemaphoreType.DMA((2,))],
        compiler_params=_cparams("arbitrary"),
        name="moe_combine_layernorm",
    )(dest_blocks, dest_blocks, y_rows, x, x16, w_gate_s, w_up_s, w_down_s, g, b)


def _dispatch_plan(idx_t, wt_t):
    t = idx_t.shape[1]
    tk = t * TOP_K
    n_blocks = tk // MOE_ROWS + N_EXPERTS
    n_rows = n_blocks * MOE_ROWS
    flat_e = idx_t.T.reshape(-1)
    flat_w = wt_t.T.reshape(-1)
    order = jnp.argsort(flat_e).astype(jnp.int32)
    counts = jnp.sum(flat_e[:, None] == jnp.arange(N_EXPERTS, dtype=jnp.int32)[None, :],
                     axis=0, dtype=jnp.int32)
    padded = (counts + MOE_ROWS - 1) // MOE_ROWS * MOE_ROWS
    pad_end = jnp.cumsum(padded)
    pad_start = pad_end - padded
    start = jnp.cumsum(counts) - counts
    block_first = jnp.arange(n_blocks, dtype=jnp.int32) * MOE_ROWS
    block_e = jnp.minimum(jnp.searchsorted(pad_end, block_first, side="right"),
                          N_EXPERTS - 1).astype(jnp.int32)
    n_active = (pad_end[-1] // MOE_ROWS).astype(jnp.int32).reshape(1)
    row = jnp.arange(n_rows, dtype=jnp.int32)
    row_e = jnp.repeat(block_e, MOE_ROWS)
    within = row - pad_start[row_e]
    valid = (within < counts[row_e]) & (row < pad_end[-1])
    src = order[jnp.clip(start[row_e] + within, 0, tk - 1)]
    row_tok = jnp.where(valid, src // TOP_K, 0).astype(jnp.int32)
    row_w = jnp.where(valid, flat_w[src], 0.0).astype(F32)
    inv = jnp.argsort(order).astype(jnp.int32)
    se = flat_e
    dest = (pad_start[se] + inv - start[se]).astype(jnp.int32)
    return block_e, n_active, row_tok.reshape(n_blocks, 1, MOE_ROWS), row_w.reshape(n_rows, 1), dest


def _layer(x, x16, cosf, sinf, perm, expand, bsz, s, p):
    t = bsz * s
    zs = _matmul(x16, p["w_z"], p["zero_z"], "silu", BF16, 1024, 512)
    xbc = _matmul(x16, p["w_xbc"], p["zero_xbc"], "none", BF16, 1024, 512)
    dt = _matmul(x16, p["w_dt"], p["dt_bias"], "softplus", F32, 1024, HEAD_PAD)
    qkv = _matmul(x16, p["w_qkv"], p["zero_qkv"], "none", BF16, 1024, 512)
    gates = _matmul(x16, p["w_gates"], p["gate_b"], "sigmoid", BF16, 1024, 512)

    y_ssd = _ssd(xbc.reshape(bsz, s, CONV_DIM), zs.reshape(bsz, s, D_INNER),
                 dt.reshape(bsz, s, HEAD_PAD), p["conv_w"], p["conv_b"], p["a_log"], p["d_skip"],
                 p["norm_w"], expand).reshape(t, D_INNER)

    qkv3 = qkv.reshape(bsz, s, 3 * ATT_WIDTH)
    outs, lses = [], []
    for gi, (_, dilation) in enumerate(ATT_GROUPS):
        o, l = _attention_group(qkv3, cosf, sinf, perm, gi, dilation)
        outs.append(o)
        lses.append(l)

    merged = _merge(y_ssd, outs, lses, p["w_ssd_br"], p["w_att_br"], gates)
    x1, x1_16 = _proj_ln(merged, p["w_o"], x, p["ln1_g"], p["ln1_b"])

    idx_t, wt_t = _router(x1, p["w_router_t"], p["router_bias"])
    block_e, n_active, row_tok, row_w, dest = _dispatch_plan(idx_t, wt_t)
    y_rows = _experts(x1, block_e, n_active, row_tok, row_w, p["w_gate_e"], p["w_up_e"], p["w_down_e"])
    dest_blocks = dest.reshape(t // COMBINE_TOKENS, COMBINE_TOKENS, TOP_K).transpose(0, 2, 1)
    dest_blocks = dest_blocks.reshape(t // COMBINE_TOKENS, 1, TOP_K * COMBINE_TOKENS)
    return _combine(y_rows, dest_blocks, x1, x1_16, p["w_gate_s"], p["w_up_s"], p["w_down_s"],
                    p["ln2_g"], p["ln2_b"])


def _pad_heads(v):
    return jnp.pad(v.astype(F32), (0, HEAD_PAD - SSD_HEADS)).reshape(1, HEAD_PAD)


def _layer_params(l, w_in, conv_w, conv_b, dt_bias, a_log, d_skip, ssd_norm_w, w_ssd_br, w_att_br,
                  gate_b, w_o, ln1_g, ln1_b, w_router, router_bias, w_gate_e, w_up_e, w_down_e,
                  w_gate_s, w_up_s, w_down_s, ln2_g, ln2_b):
    offs = [0]
    for size in IN_SIZES:
        offs.append(offs[-1] + size)
    wi = w_in[l]
    row = lambda v: v.astype(F32).reshape(1, -1)
    return dict(
        w_z=wi[:, offs[0]:offs[1]].astype(BF16),
        w_xbc=wi[:, offs[1]:offs[2]].astype(BF16),
        w_dt=jnp.pad(wi[:, offs[2]:offs[3]], ((0, 0), (0, HEAD_PAD - SSD_HEADS))).astype(BF16),
        w_qkv=wi[:, offs[3]:offs[6]].astype(BF16),
        w_gates=wi[:, offs[6]:offs[8]].astype(BF16),
        zero_z=jnp.zeros((1, D_INNER), F32), zero_xbc=jnp.zeros((1, CONV_DIM), F32),
        zero_qkv=jnp.zeros((1, 3 * ATT_WIDTH), F32),
        dt_bias=_pad_heads(dt_bias[l]), gate_b=row(gate_b[l]),
        conv_w=conv_w[l].astype(F32), conv_b=row(conv_b[l]),
        a_log=_pad_heads(a_log[l]), d_skip=_pad_heads(d_skip[l]), norm_w=row(ssd_norm_w[l]),
        w_ssd_br=w_ssd_br[l].astype(BF16), w_att_br=w_att_br[l].astype(BF16),
        w_o=w_o[l].astype(BF16), ln1_g=row(ln1_g[l]), ln1_b=row(ln1_b[l]),
        w_router_t=w_router[l].astype(F32).T, router_bias=router_bias[l].astype(F32).reshape(-1, 1),
        w_gate_e=w_gate_e[l].astype(BF16), w_up_e=w_up_e[l].astype(BF16),
        w_down_e=w_down_e[l].astype(BF16),
        w_gate_s=w_gate_s[l].astype(BF16), w_up_s=w_up_s[l].astype(BF16),
        w_down_s=w_down_s[l].astype(BF16), ln2_g=row(ln2_g[l]), ln2_b=row(ln2_b[l]))


def _rope_tables(positions):
    half = ROPE_DIM // 2
    inv_freq = ROPE_THETA ** (-jnp.arange(0, ROPE_DIM, 2, dtype=F32) / ROPE_DIM)
    ang = positions.astype(F32)[..., None] * inv_freq
    cos, sin = jnp.cos(ang), jnp.sin(ang)
    pad = positions.shape + (ATT_HEAD_DIM - ROPE_DIM,)
    cosf = jnp.concatenate([cos, cos, jnp.ones(pad, F32)], axis=-1)
    sinf = jnp.concatenate([-sin, sin, jnp.zeros(pad, F32)], axis=-1)
    k = jnp.arange(ATT_HEAD_DIM)[:, None]
    j = jnp.arange(ATT_HEAD_DIM)[None, :]
    perm = ((j < half) & (k == j + half)) | ((j >= half) & (j < ROPE_DIM) & (k == j - half))
    return cosf, sinf, perm.astype(BF16)


def kernel(x, positions, w_in, conv_w, conv_b, dt_bias, a_log, d_skip, ssd_norm_w, w_ssd_br,
           w_att_br, gate_b, w_o, ln1_g, ln1_b, w_router, router_bias, w_gate_e, w_up_e, w_down_e,
           w_gate_s, w_up_s, w_down_s, ln2_g, ln2_b):
    bsz, s, d = x.shape
    cosf, sinf, perm = _rope_tables(positions)
    head_of_col = jnp.arange(D_INNER) // SSD_HEAD_DIM
    expand = (jnp.arange(HEAD_PAD)[:, None] == head_of_col[None, :]).astype(BF16)
    xf = x.reshape(bsz * s, d).astype(F32)
    x16 = xf.astype(BF16)
    for l in range(DEPTH):
        p = _layer_params(l, w_in, conv_w, conv_b, dt_bias, a_log, d_skip, ssd_norm_w, w_ssd_br,
                          w_att_br, gate_b, w_o, ln1_g, ln1_b, w_router, router_bias, w_gate_e,
                          w_up_e, w_down_e, w_gate_s, w_up_s, w_down_s, ln2_g, ln2_b)
        xf, x16 = _layer(xf, x16, cosf, sinf, perm, expand, bsz, s, p)
    return xf.reshape(bsz, s, d).astype(x.dtype)
```

```python
import functools

import jax
import jax.numpy as jnp
from jax import lax
from jax.experimental import pallas as pl
from jax.experimental.pallas import tpu as pltpu

D_MODEL = 2048
DEPTH = 2

D_INNER = 4096
SSD_HEAD_DIM = 64
SSD_HEADS = 64
SSD_GROUPS = 8
SSD_HEADS_PER_GROUP = 8
SSD_STATE = 128
SSD_CONV = 4
SSD_CHUNK = 128
CONV_DIM = D_INNER + 2 * SSD_GROUPS * SSD_STATE
RMS_EPS = 1e-5

ATT_HEAD_DIM = 128
ATT_GROUPS = ((128, 1), (512, 4), (2048, 16))
ATT_HEADS_PER_GROUP = 4
ATT_WIDTH = 1536
ATT_OUT = 512
ATT_BLOCK = 128
ROPE_THETA = 500000.0
ROPE_DIM = 32

IN_SIZES = (D_INNER, CONV_DIM, SSD_HEADS, ATT_WIDTH, ATT_WIDTH, ATT_WIDTH, D_MODEL, D_MODEL)
DT_COL = D_INNER + CONV_DIM
TAIL_COL = DT_COL + SSD_HEADS

N_EXPERTS = 64
N_EXPERT_GROUPS = 8
EXPERTS_PER_GROUP = 8
TOPK_GROUPS = 4
TOP_K = 8
D_EXPERT = 512
D_SHARED = 512
ROUTED_SCALE = 2.5

DEEPNORM_ALPHA = (2 * DEPTH) ** 0.25
LN_EPS = 1e-5

LANES = 128
HEAD_PAD = 128
MOE_ROWS = 256
COMBINE_TOKENS = 128
VMEM_LIMIT = 56 * 1024 * 1024

F32 = jnp.float32
BF16 = jnp.bfloat16


def _cparams(*sem):
    return pltpu.CompilerParams(dimension_semantics=sem, vmem_limit_bytes=VMEM_LIMIT)


def _sigmoid(v):
    return 1.0 / (1.0 + jnp.exp(-v))


def _mm_kernel(x_ref, w_ref, b_ref, o_ref, w16_ref, *, act):
    @pl.when(pl.program_id(1) == 0)
    def _():
        w16_ref[...] = w_ref[0].astype(BF16)

    acc = jnp.dot(x_ref[...], w16_ref[...], preferred_element_type=F32) + b_ref[...]
    if act == "silu":
        acc = acc * _sigmoid(acc)
    elif act == "sigmoid":
        acc = _sigmoid(acc)
    elif act == "softplus":
        acc = jnp.maximum(acc, 0.0) + jnp.log(1.0 + jnp.exp(-jnp.abs(acc)))
    o_ref[...] = acc.astype(o_ref.dtype)


def _matmul(x, w, layer, col0, n, bias, act, out_dtype, tm, tn):
    m, k = x.shape
    assert col0 % tn == 0 and n % tn == 0 and m % tm == 0
    j0 = col0 // tn
    return pl.pallas_call(
        functools.partial(_mm_kernel, act=act),
        out_shape=jax.ShapeDtypeStruct((m, n), out_dtype),
        grid=(n // tn, m // tm),
        in_specs=[pl.BlockSpec((tm, k), lambda j, i: (i, 0)),
                  pl.BlockSpec((1, k, tn), lambda j, i: (layer, 0, j0 + j)),
                  pl.BlockSpec((1, tn), lambda j, i: (0, j))],
        out_specs=pl.BlockSpec((tm, tn), lambda j, i: (i, j)),
        scratch_shapes=[pltpu.VMEM((k, tn), BF16)],
        compiler_params=_cparams("parallel", "arbitrary"),
        name="proj_" + act,
    )(x, w, bias)


def _split_dot(v, e):
    hi = v.astype(BF16)
    lo = (v - hi.astype(F32)).astype(BF16)
    return (jnp.dot(hi, e, preferred_element_type=F32)
            + jnp.dot(lo, e, preferred_element_type=F32))


def _split_dot_left(e, v):
    hi = v.astype(BF16)
    lo = (v - hi.astype(F32)).astype(BF16)
    return (jnp.dot(e, hi, preferred_element_type=F32)
            + jnp.dot(e, lo, preferred_element_type=F32))


def _ssd_kernel(xbc_ref, zs_ref, dt_ref, convw_ref, convb_ref, alog_ref, dskip_ref, normw_ref,
                expand_ref, o_ref, xpad_ref, state_ref, y_ref, dtx_ref, dtex_ref, eax_ref):
    q = SSD_CHUNK
    gw = SSD_HEADS_PER_GROUP * SSD_HEAD_DIM

    @pl.when(pl.program_id(1) == 0)
    def _():
        xpad_ref[0:8, :] = jnp.zeros((8, CONV_DIM), F32)
        state_ref[...] = jnp.zeros_like(state_ref)

    xpad_ref[8:8 + q, :] = xbc_ref[0].astype(F32)

    def conv(c0, width):
        acc = convb_ref[:, c0:c0 + width]
        for j in range(SSD_CONV):
            lo = 8 - (SSD_CONV - 1) + j
            acc = acc + convw_ref[j:j + 1, c0:c0 + width] * xpad_ref[lo:lo + q, c0:c0 + width]
        return acc * _sigmoid(acc)

    row = lax.broadcasted_iota(jnp.int32, (q, q), 0)
    col = lax.broadcasted_iota(jnp.int32, (q, q), 1)
    causal = col <= row
    tril = jnp.where(causal, 1.0, 0.0).astype(BF16)

    dt = dt_ref[0]
    a = -jnp.exp(alog_ref[...])
    da = dt * a
    a_cs = _split_dot_left(tril, da)
    a_cs_t = a_cs.T
    a_last = a_cs[q - 1:q, :]
    expand = expand_ref[...]
    dtx_ref[...] = _split_dot(dt, expand)
    dtex_ref[...] = _split_dot(dt * jnp.exp(a_last - a_cs), expand)
    eax_ref[...] = _split_dot(jnp.exp(a_cs), expand)
    decx = _split_dot(jnp.broadcast_to(jnp.exp(a_last), (8, HEAD_PAD)), expand)[0:1, :]
    dskipx = _split_dot(jnp.broadcast_to(dskip_ref[...], (8, HEAD_PAD)), expand)[0:1, :]

    for g in range(SSD_GROUPS):
        c0 = g * gw
        xs = conv(c0, gw)
        bm = conv(D_INNER + g * SSD_STATE, SSD_STATE)
        cm = conv(D_INNER + SSD_GROUPS * SSD_STATE + g * SSD_STATE, SSD_STATE)
        bm_t = bm.T.astype(BF16)
        cm16 = cm.astype(BF16)
        cb = jnp.dot(cm16, bm_t, preferred_element_type=F32)
        xdt = (xs * dtx_ref[:, c0:c0 + gw]).astype(BF16)
        for r in range(SSD_HEADS_PER_GROUP):
            h = g * SSD_HEADS_PER_GROUP + r
            seg = a_cs[:, h:h + 1] - a_cs_t[h:h + 1, :]
            decay = jnp.exp(jnp.where(causal, seg, -jnp.inf))
            m = (cb * decay).astype(BF16)
            y_ref[:, h * SSD_HEAD_DIM:(h + 1) * SSD_HEAD_DIM] = jnp.dot(
                m, xdt[:, r * SSD_HEAD_DIM:(r + 1) * SSD_HEAD_DIM], preferred_element_type=F32)
        st = state_ref[:, c0:c0 + gw]
        y_off = jnp.dot(cm16, st.astype(BF16), preferred_element_type=F32) * eax_ref[:, c0:c0 + gw]
        xdte = (xs * dtex_ref[:, c0:c0 + gw]).astype(BF16)
        state_ref[:, c0:c0 + gw] = decx[:, c0:c0 + gw] * st + jnp.dot(
            bm_t, xdte, preferred_element_type=F32)
        y = y_ref[:, c0:c0 + gw] + y_off + dskipx[:, c0:c0 + gw] * xs
        y = y * zs_ref[0, :, c0:c0 + gw].astype(F32)
        ms = jnp.mean(y * y, axis=-1, keepdims=True)
        o_ref[0, :, c0:c0 + gw] = (y * lax.rsqrt(ms + RMS_EPS)
                                   * normw_ref[:, c0:c0 + gw]).astype(o_ref.dtype)

    xpad_ref[0:8, :] = xpad_ref[q:q + 8, :]


def _ssd(xbc, zs, dt, conv_w, conv_b, a_log, d_skip, norm_w, expand):
    bsz, s, _ = xbc.shape
    nc = s // SSD_CHUNK
    full = lambda shape: pl.BlockSpec(shape, lambda b, c: (0, 0))
    return pl.pallas_call(
        _ssd_kernel,
        out_shape=jax.ShapeDtypeStruct((bsz, s, D_INNER), BF16),
        grid=(bsz, nc),
        in_specs=[pl.BlockSpec((1, SSD_CHUNK, CONV_DIM), lambda b, c: (b, c, 0)),
                  pl.BlockSpec((1, SSD_CHUNK, D_INNER), lambda b, c: (b, c, 0)),
                  pl.BlockSpec((1, SSD_CHUNK, HEAD_PAD), lambda b, c: (b, c, 0)),
                  full((SSD_CONV, CONV_DIM)), full((1, CONV_DIM)), full((1, HEAD_PAD)),
                  full((1, HEAD_PAD)), full((1, D_INNER)), full((HEAD_PAD, D_INNER))],
        out_specs=pl.BlockSpec((1, SSD_CHUNK, D_INNER), lambda b, c: (b, c, 0)),
        scratch_shapes=[pltpu.VMEM((SSD_CHUNK + 8, CONV_DIM), F32),
                        pltpu.VMEM((SSD_STATE, D_INNER), F32),
                        pltpu.VMEM((SSD_CHUNK, D_INNER), F32),
                        pltpu.VMEM((SSD_CHUNK, D_INNER), F32),
                        pltpu.VMEM((SSD_CHUNK, D_INNER), F32),
                        pltpu.VMEM((SSD_CHUNK, D_INNER), F32)],
        compiler_params=_cparams("parallel", "arbitrary"),
        name="ssd_mixer",
    )(xbc, zs, dt, conv_w, conv_b, a_log, d_skip, norm_w, expand)


ATT_STAGE_ROWS = 256


def _attn_kernel(q_ref, k_ref, v_ref, cos_ref, sin_ref, perm_ref, o_ref, lse_ref,
                 q32_ref, k32_ref, v32_ref, *, dilation):
    s = q_ref.shape[1]
    blk, dh, d = ATT_BLOCK, ATT_HEAD_DIM, dilation
    nb = s // d // blk
    perm = perm_ref[...]

    def stage(c, carry):
        rows = pl.ds(pl.multiple_of(c * ATT_STAGE_ROWS, ATT_STAGE_ROWS), ATT_STAGE_ROWS)
        cos, sin = cos_ref[0, rows, :], sin_ref[0, rows, :]
        for h in range(ATT_HEADS_PER_GROUP):
            sl = slice(h * dh, (h + 1) * dh)
            qh, kh = q_ref[0, rows, sl], k_ref[0, rows, sl]
            q_rot = qh.astype(F32) * cos + jnp.dot(qh, perm, preferred_element_type=F32) * sin
            q32_ref[h, rows, :] = q_rot * (dh ** -0.5)
            k32_ref[h, rows, :] = (kh.astype(F32) * cos
                                   + jnp.dot(kh, perm, preferred_element_type=F32) * sin)
            v32_ref[h, rows, :] = v_ref[0, rows, sl].astype(F32)
        return carry

    lax.fori_loop(0, s // ATT_STAGE_ROWS, stage, 0)

    qi = lax.broadcasted_iota(jnp.int32, (blk, blk), 0)
    kj = lax.broadcasted_iota(jnp.int32, (blk, blk), 1)
    mask_c = kj <= qi
    mask_p = kj >= qi
    nt = (((1,), (1,)), ((), ()))

    def sub_rows(start):
        return pl.ds(start, blk, stride=d) if d > 1 else pl.ds(start, blk)

    def head(h, carry):
        for r in range(d):
            for n in range(nb):
                rows = sub_rows(r + d * blk * n)
                qb = q32_ref[h, rows, :].astype(BF16)
                kc = k32_ref[h, rows, :].astype(BF16)
                vc = v32_ref[h, rows, :].astype(BF16)
                s_c = jnp.where(mask_c, lax.dot_general(qb, kc, nt, preferred_element_type=F32),
                                -jnp.inf)
                m = jnp.max(s_c, axis=-1, keepdims=True)
                if n > 0:
                    prev = sub_rows(r + d * blk * (n - 1))
                    kp = k32_ref[h, prev, :].astype(BF16)
                    vp = v32_ref[h, prev, :].astype(BF16)
                    s_p = jnp.where(mask_p, lax.dot_general(qb, kp, nt, preferred_element_type=F32),
                                    -jnp.inf)
                    m = jnp.maximum(m, jnp.max(s_p, axis=-1, keepdims=True))
                p_c = jnp.exp(s_c - m)
                den = jnp.sum(p_c, axis=-1, keepdims=True)
                acc = jnp.dot(p_c.astype(BF16), vc, preferred_element_type=F32)
                if n > 0:
                    p_p = jnp.exp(s_p - m)
                    den = den + jnp.sum(p_p, axis=-1, keepdims=True)
                    acc = acc + jnp.dot(p_p.astype(BF16), vp, preferred_element_type=F32)
                o_ref[0, h, rows, :] = acc / den
                lse_ref[0, h, rows, :] = jnp.broadcast_to(m + jnp.log(den), (blk, dh))
        return carry

    lax.fori_loop(0, ATT_HEADS_PER_GROUP, head, 0)


def _attention_group(qkv, cosf, sinf, perm, gi, dilation):
    bsz, s, _ = qkv.shape
    gw = ATT_HEADS_PER_GROUP * ATT_HEAD_DIM
    col = lambda which: pl.BlockSpec((1, s, gw), lambda b: (b, 0, which * 3 + gi))
    tab = pl.BlockSpec((1, s, LANES), lambda b: (b, 0, 0))
    out_spec = pl.BlockSpec((1, ATT_HEADS_PER_GROUP, s, ATT_HEAD_DIM), lambda b: (b, 0, 0, 0))
    shape = jax.ShapeDtypeStruct((bsz, ATT_HEADS_PER_GROUP, s, ATT_HEAD_DIM), F32)
    scratch = pltpu.VMEM((ATT_HEADS_PER_GROUP, s, ATT_HEAD_DIM), F32)
    return pl.pallas_call(
        functools.partial(_attn_kernel, dilation=dilation),
        out_shape=(shape, shape),
        grid=(bsz,),
        in_specs=[col(0), col(1), col(2), tab, tab, pl.BlockSpec((LANES, LANES), lambda b: (0, 0))],
        out_specs=(out_spec, out_spec),
        scratch_shapes=[scratch, scratch, scratch],
        compiler_params=_cparams("parallel"),
        name="dilated_attention_%d" % dilation,
    )(qkv, qkv, qkv, cosf, sinf, perm)


def _merge_kernel(y_ref, o1_ref, o2_ref, o3_ref, l1_ref, l2_ref, l3_ref, ws_ref, wa_ref,
                  gs_ref, ga_ref, out_ref, att_ref):
    @pl.when(pl.program_id(1) == 0)
    def _():
        for h in range(ATT_HEADS_PER_GROUP):
            l1, l2, l3 = l1_ref[0, h], l2_ref[0, h], l3_ref[0, h]
            m = jnp.maximum(jnp.maximum(l1, l2), l3)
            e1, e2, e3 = jnp.exp(l1 - m), jnp.exp(l2 - m), jnp.exp(l3 - m)
            num = e1 * o1_ref[0, h] + e2 * o2_ref[0, h] + e3 * o3_ref[0, h]
            att_ref[:, h * ATT_HEAD_DIM:(h + 1) * ATT_HEAD_DIM] = (num / (e1 + e2 + e3)).astype(BF16)

    y_ssd = jnp.dot(y_ref[...], ws_ref[...], preferred_element_type=F32)
    y_att = jnp.dot(att_ref[...], wa_ref[...], preferred_element_type=F32)
    out_ref[...] = (gs_ref[...].astype(F32) * y_ssd
                    + ga_ref[...].astype(F32) * y_att).astype(out_ref.dtype)


def _merge(y_ssd, outs, lses, w_ssd_br, w_att_br, gates, s, tm=512, tn=512):
    t = y_ssd.shape[0]
    nj = D_MODEL // tn
    per_batch = s // tm
    head_major = pl.BlockSpec((1, ATT_HEADS_PER_GROUP, tm, ATT_HEAD_DIM),
                              lambda i, j: (i // per_batch, 0, i % per_batch, 0))
    return pl.pallas_call(
        _merge_kernel,
        out_shape=jax.ShapeDtypeStruct((t, D_MODEL), BF16),
        grid=(t // tm, nj),
        in_specs=[pl.BlockSpec((tm, D_INNER), lambda i, j: (i, 0))] + [head_major] * 6
                 + [pl.BlockSpec((D_INNER, tn), lambda i, j: (0, j)),
                    pl.BlockSpec((ATT_OUT, tn), lambda i, j: (0, j)),
                    pl.BlockSpec((tm, tn), lambda i, j: (i, j)),
                    pl.BlockSpec((tm, tn), lambda i, j: (i, j + nj))],
        out_specs=pl.BlockSpec((tm, tn), lambda i, j: (i, j)),
        scratch_shapes=[pltpu.VMEM((tm, ATT_OUT), BF16)],
        compiler_params=_cparams("parallel", "arbitrary"),
        name="branch_merge",
    )(y_ssd, *outs, *lses, w_ssd_br, w_att_br, gates, gates)


def _layer_norm(v, g, b):
    mu = jnp.mean(v, axis=-1, keepdims=True)
    c = v - mu
    var = jnp.mean(c * c, axis=-1, keepdims=True)
    return c * lax.rsqrt(var + LN_EPS) * g + b


def _proj_ln_kernel(m_ref, w_ref, x_ref, g_ref, b_ref, o_ref, o16_ref):
    v = DEEPNORM_ALPHA * x_ref[...] + jnp.dot(m_ref[...], w_ref[...], preferred_element_type=F32)
    out = _layer_norm(v, g_ref[...], b_ref[...])
    o_ref[...] = out
    o16_ref[...] = out.astype(BF16)


def _proj_ln(merged, w_o, x, g, b, tm=256):
    t = x.shape[0]
    row = lambda: pl.BlockSpec((tm, D_MODEL), lambda i: (i, 0))
    const = lambda shape: pl.BlockSpec(shape, lambda i: (0, 0))
    return pl.pallas_call(
        _proj_ln_kernel,
        out_shape=(jax.ShapeDtypeStruct((t, D_MODEL), F32), jax.ShapeDtypeStruct((t, D_MODEL), BF16)),
        grid=(t // tm,),
        in_specs=[row(), const((D_MODEL, D_MODEL)), row(), const((1, D_MODEL)), const((1, D_MODEL))],
        out_specs=(row(), row()),
        compiler_params=_cparams("parallel"),
        name="out_proj_layernorm",
    )(merged, w_o, x, g, b)


def _router_kernel(x_ref, w_ref, bias_ref, idx_ref, wt_ref, rank_ref, cnt_ref, carry_ref):
    tm = x_ref.shape[0]
    ne, ng, eg = N_EXPERTS, N_EXPERT_GROUPS, EXPERTS_PER_GROUP

    @pl.when(pl.program_id(0) == 0)
    def _():
        carry_ref[...] = jnp.zeros_like(carry_ref)

    logits = lax.dot_general(w_ref[...], x_ref[...], (((1,), (1,)), ((), ())),
                             precision=lax.Precision.HIGHEST, preferred_element_type=F32)
    scores = _sigmoid(logits)
    sel = scores + bias_ref[...]
    sub = lax.broadcasted_iota(jnp.int32, (eg, tm), 0)
    neg = -jnp.inf

    gscore = []
    for g in range(ng):
        sg = sel[g * eg:(g + 1) * eg, :]
        m1 = jnp.max(sg, axis=0, keepdims=True)
        first = jnp.min(jnp.where(sg == m1, sub, eg), axis=0, keepdims=True)
        m2 = jnp.max(jnp.where(sub == first, neg, sg), axis=0, keepdims=True)
        gscore.append(m1 + m2)
    keep = []
    for g in range(ng):
        beaten = jnp.zeros((1, tm), jnp.int32)
        for o in range(ng):
            if o == g:
                continue
            wins = (gscore[o] >= gscore[g]) if o < g else (gscore[o] > gscore[g])
            beaten = beaten + wins.astype(jnp.int32)
        keep.append(beaten < TOPK_GROUPS)
    masked = jnp.concatenate(
        [jnp.where(keep[g], sel[g * eg:(g + 1) * eg, :], neg) for g in range(ng)], axis=0)

    eidx = lax.broadcasted_iota(jnp.int32, (ne, tm), 0)
    picks, weights, chosen_masks = [], [], []
    for _ in range(TOP_K):
        mx = jnp.max(masked, axis=0, keepdims=True)
        pick = jnp.min(jnp.where(masked == mx, eidx, ne), axis=0, keepdims=True)
        chosen = eidx == pick
        weights.append(jnp.sum(jnp.where(chosen, scores, 0.0), axis=0, keepdims=True))
        picks.append(pick)
        chosen_masks.append(chosen)
        masked = jnp.where(chosen, neg, masked)
    w = jnp.concatenate(weights, axis=0)
    idx_ref[...] = jnp.concatenate(picks, axis=0)
    wt_ref[...] = w / jnp.sum(w, axis=0, keepdims=True) * ROUTED_SCALE

    cnt = jnp.zeros((ne, tm), F32)
    for chosen in chosen_masks:
        cnt = cnt + jnp.where(chosen, 1.0, 0.0)
    earlier = (lax.broadcasted_iota(jnp.int32, (tm, tm), 0)
               < lax.broadcasted_iota(jnp.int32, (tm, tm), 1))
    before = jnp.dot(cnt.astype(BF16), jnp.where(earlier, 1.0, 0.0).astype(BF16),
                     preferred_element_type=F32) + carry_ref[...]
    rank_ref[...] = jnp.concatenate(
        [jnp.sum(jnp.where(chosen, before, 0.0), axis=0, keepdims=True) for chosen in chosen_masks],
        axis=0).astype(jnp.int32)
    carry_ref[...] = carry_ref[...] + jnp.sum(cnt, axis=1, keepdims=True)
    cnt_ref[...] = carry_ref[...]


def _router(x, w_router_t, bias, tm=512):
    t = x.shape[0]
    tok = lambda: pl.BlockSpec((TOP_K, tm), lambda i: (0, i))
    return pl.pallas_call(
        _router_kernel,
        out_shape=(jax.ShapeDtypeStruct((TOP_K, t), jnp.int32),
                   jax.ShapeDtypeStruct((TOP_K, t), F32),
                   jax.ShapeDtypeStruct((TOP_K, t), jnp.int32),
                   jax.ShapeDtypeStruct((N_EXPERTS, 1), F32)),
        grid=(t // tm,),
        in_specs=[pl.BlockSpec((tm, D_MODEL), lambda i: (i, 0)),
                  pl.BlockSpec((N_EXPERTS, D_MODEL), lambda i: (0, 0)),
                  pl.BlockSpec((N_EXPERTS, 1), lambda i: (0, 0))],
        out_specs=(tok(), tok(), tok(), pl.BlockSpec((N_EXPERTS, 1), lambda i: (0, 0))),
        scratch_shapes=[pltpu.VMEM((N_EXPERTS, 1), F32)],
        compiler_params=_cparams("arbitrary"),
        name="router_topk",
    )(x, w_router_t, bias)


def _gather_rows(src_hbm, idx_ref, dst_ref, sem, n_rows):
    def body(r, carry):
        pltpu.make_async_copy(src_hbm.at[pl.ds(idx_ref[0, 0, r], 1), :],
                              dst_ref.at[pl.ds(r, 1), :], sem).start()
        return carry
    lax.fori_loop(0, n_rows, body, 0, unroll=8)


def _experts_kernel(be_ref, na_ref, tok0_ref, tokn_ref, x_hbm, wrow_ref, wg_ref, wu_ref, wd_ref,
                    y_ref, buf_ref, sem_ref, wg16_ref, wu16_ref, wd16_ref):
    i = pl.program_id(0)
    n_active = na_ref[0]
    slot = lax.rem(i, 2)

    @pl.when(i == 0)
    def _():
        _gather_rows(x_hbm, tok0_ref, buf_ref.at[0], sem_ref.at[0], MOE_ROWS)

    @pl.when(i + 1 < n_active)
    def _():
        _gather_rows(x_hbm, tokn_ref, buf_ref.at[1 - slot], sem_ref.at[1 - slot], MOE_ROWS)

    new_expert = (i == 0) | (be_ref[i] != be_ref[jnp.maximum(i - 1, 0)])

    @pl.when(new_expert & (i < n_active))
    def _():
        wg16_ref[...] = wg_ref[0].astype(BF16)
        wu16_ref[...] = wu_ref[0].astype(BF16)
        wd16_ref[...] = wd_ref[0].astype(BF16)

    @pl.when(i < n_active)
    def _():
        pltpu.make_async_copy(x_hbm.at[pl.ds(0, MOE_ROWS), :], buf_ref.at[slot],
                              sem_ref.at[slot]).wait()
        xb = buf_ref[slot].astype(BF16)
        gate = jnp.dot(xb, wg16_ref[...], preferred_element_type=F32)
        up = jnp.dot(xb, wu16_ref[...], preferred_element_type=F32)
        h = gate * _sigmoid(gate) * up * wrow_ref[...]
        y_ref[...] = jnp.dot(h.astype(BF16), wd16_ref[...], preferred_element_type=F32)

    @pl.when(i >= n_active)
    def _():
        y_ref[...] = jnp.zeros_like(y_ref)


def _experts(x, block_e, n_active, row_tok, row_w, w_gate, w_up, w_down, layer):
    n_blocks = row_tok.shape[0]
    n_rows = n_blocks * MOE_ROWS
    smem_blk = lambda index_map: pl.BlockSpec((1, 1, MOE_ROWS), index_map, memory_space=pltpu.SMEM)
    wspec = lambda a, b: pl.BlockSpec((1, 1, a, b), lambda i, be, na: (layer, be[i], 0, 0))

    def kernel(be_ref, na_ref, tok0_ref, tokn_ref, x_hbm, wrow_ref, wg_ref, wu_ref, wd_ref, *rest):
        _experts_kernel(be_ref, na_ref, tok0_ref, tokn_ref, x_hbm, wrow_ref, wg_ref.at[0],
                        wu_ref.at[0], wd_ref.at[0], *rest)

    grid_spec = pltpu.PrefetchScalarGridSpec(
        num_scalar_prefetch=2,
        grid=(n_blocks,),
        in_specs=[smem_blk(lambda i, be, na: (0, 0, 0)),
                  smem_blk(lambda i, be, na: (jnp.minimum(i + 1, n_blocks - 1), 0, 0)),
                  pl.BlockSpec(memory_space=pl.ANY),
                  pl.BlockSpec((MOE_ROWS, 1), lambda i, be, na: (i, 0)),
                  wspec(D_MODEL, D_EXPERT), wspec(D_MODEL, D_EXPERT), wspec(D_EXPERT, D_MODEL)],
        out_specs=pl.BlockSpec((MOE_ROWS, D_MODEL), lambda i, be, na: (i, 0)),
        scratch_shapes=[pltpu.VMEM((2, MOE_ROWS, D_MODEL), F32),
                        pltpu.SemaphoreType.DMA((2,)),
                        pltpu.VMEM((D_MODEL, D_EXPERT), BF16),
                        pltpu.VMEM((D_MODEL, D_EXPERT), BF16),
                        pltpu.VMEM((D_EXPERT, D_MODEL), BF16)])
    return pl.pallas_call(
        kernel,
        out_shape=jax.ShapeDtypeStruct((n_rows, D_MODEL), F32),
        grid_spec=grid_spec,
        compiler_params=_cparams("arbitrary"),
        name="routed_experts",
    )(block_e, n_active, row_tok, row_tok, x, row_w, w_gate, w_up, w_down)


def _combine_kernel(d0_ref, dn_ref, y_hbm, x_ref, x16_ref, wg_ref, wu_ref, wd_ref, g_ref, b_ref,
                    o_ref, o16_ref, buf_ref, sem_ref):
    i = pl.program_id(0)
    n = pl.num_programs(0)
    slot = lax.rem(i, 2)
    rows = COMBINE_TOKENS * TOP_K

    @pl.when(i == 0)
    def _():
        _gather_rows(y_hbm, d0_ref, buf_ref.at[0], sem_ref.at[0], rows)

    @pl.when(i + 1 < n)
    def _():
        _gather_rows(y_hbm, dn_ref, buf_ref.at[1 - slot], sem_ref.at[1 - slot], rows)

    x16 = x16_ref[...]
    gate = jnp.dot(x16, wg_ref[...], preferred_element_type=F32)
    up = jnp.dot(x16, wu_ref[...], preferred_element_type=F32)
    h = (gate * _sigmoid(gate) * up).astype(BF16)
    v = DEEPNORM_ALPHA * x_ref[...] + jnp.dot(h, wd_ref[...], preferred_element_type=F32)

    pltpu.make_async_copy(y_hbm.at[pl.ds(0, rows), :], buf_ref.at[slot], sem_ref.at[slot]).wait()
    for k in range(TOP_K):
        v = v + buf_ref[slot, k * COMBINE_TOKENS:(k + 1) * COMBINE_TOKENS, :]
    out = _layer_norm(v, g_ref[...], b_ref[...])
    o_ref[...] = out
    o16_ref[...] = out.astype(BF16)


def _combine(y_rows, dest_blocks, x, x16, w_gate_s, w_up_s, w_down_s, g, b):
    t = x.shape[0]
    tm = COMBINE_TOKENS
    rows = tm * TOP_K
    row = lambda: pl.BlockSpec((tm, D_MODEL), lambda i: (i, 0))
    const = lambda shape: pl.BlockSpec(shape, lambda i: (0, 0))
    nblk = t // tm
    smem_blk = lambda index_map: pl.BlockSpec((1, 1, rows), index_map, memory_space=pltpu.SMEM)
    return pl.pallas_call(
        _combine_kernel,
        out_shape=(jax.ShapeDtypeStruct((t, D_MODEL), F32), jax.ShapeDtypeStruct((t, D_MODEL), BF16)),
        grid=(nblk,),
        in_specs=[smem_blk(lambda i: (0, 0, 0)),
                  smem_blk(lambda i: (jnp.minimum(i + 1, nblk - 1), 0, 0)),
                  pl.BlockSpec(memory_space=pl.ANY),
                  row(), row(),
                  const((D_MODEL, D_SHARED)), const((D_MODEL, D_SHARED)), const((D_SHARED, D_MODEL)),
                  const((1, D_MODEL)), const((1, D_MODEL))],
        out_specs=(row(), row()),
        scratch_shapes=[pltpu.VMEM((2, rows, D_MODEL), F32), pltpu.SemaphoreType.DMA((2,))],
        compiler_params=_cparams("arbitrary"),
        name="moe_combine_layernorm",
    )(dest_blocks, dest_blocks, y_rows, x, x16, w_gate_s, w_up_s, w_down_s, g, b)


def _dispatch_plan(idx_t, wt_t, rank_t, counts):
    t = idx_t.shape[1]
    tk = t * TOP_K
    n_blocks = tk // MOE_ROWS + N_EXPERTS
    experts = jnp.arange(N_EXPERTS, dtype=jnp.int32)
    counts = counts.reshape(N_EXPERTS).astype(jnp.int32)
    padded = (counts + MOE_ROWS - 1) // MOE_ROWS * MOE_ROWS
    pad_end = jnp.cumsum(padded)
    pad_start = pad_end - padded
    start = jnp.cumsum(counts) - counts
    first_row = jnp.sum(jnp.where(idx_t[None] == experts[:, None, None],
                                  pad_start[:, None, None], 0), axis=0)
    dest_t = (first_row + rank_t).astype(jnp.int32)
    tok = jnp.broadcast_to(jnp.arange(t, dtype=jnp.int32)[None, :], (TOP_K, t))
    _, sorted_tok, sorted_w = lax.sort((dest_t.reshape(-1), tok.reshape(-1), wt_t.reshape(-1)),
                                       num_keys=1)
    block_first = jnp.arange(n_blocks, dtype=jnp.int32) * MOE_ROWS
    block_e = jnp.minimum(jnp.sum(pad_end[None, :] <= block_first[:, None], axis=1),
                          N_EXPERTS - 1).astype(jnp.int32)
    n_active = (pad_end[-1] // MOE_ROWS).astype(jnp.int32).reshape(1)
    within = (block_first - pad_start[block_e])[:, None] + jnp.arange(MOE_ROWS, dtype=jnp.int32)[None, :]
    valid = (within < counts[block_e][:, None]) & (block_first < pad_end[-1])[:, None]
    pos = jnp.clip(start[block_e][:, None] + within, 0, tk - 1)
    row_tok = jnp.where(valid, sorted_tok[pos], 0).astype(jnp.int32)
    row_w = jnp.where(valid, sorted_w[pos], 0.0).astype(F32)
    return (block_e, n_active, row_tok.reshape(n_blocks, 1, MOE_ROWS),
            row_w.reshape(n_blocks * MOE_ROWS, 1), dest_t)


def _layer(x, x16, cosf, sinf, perm, expand, bsz, s, l, w_in, w_tail, w_gate_e, w_up_e, w_down_e, p):
    t = bsz * s
    zs = _matmul(x16, w_in, l, 0, D_INNER, p["zero_z"], "silu", BF16, 1024, 1024)
    xbc = _matmul(x16, w_in, l, D_INNER, CONV_DIM, p["zero_xbc"], "none", BF16, 1024, 1024)
    dt = _matmul(x16, w_in, l, DT_COL, HEAD_PAD, p["dt_bias"], "softplus", F32, 1024, HEAD_PAD)
    qkv = _matmul(x16, w_tail, l, 0, 3 * ATT_WIDTH, p["zero_qkv"], "none", BF16, 1024, 768)
    gates = _matmul(x16, w_tail, l, 3 * ATT_WIDTH, 2 * D_MODEL, p["gate_b"], "sigmoid", BF16, 1024, 512)

    y_ssd = _ssd(xbc.reshape(bsz, s, CONV_DIM), zs.reshape(bsz, s, D_INNER),
                 dt.reshape(bsz, s, HEAD_PAD), p["conv_w"], p["conv_b"], p["a_log"], p["d_skip"],
                 p["norm_w"], expand).reshape(t, D_INNER)

    qkv3 = qkv.reshape(bsz, s, 3 * ATT_WIDTH)
    outs, lses = [], []
    for gi, (_, dilation) in enumerate(ATT_GROUPS):
        o, lse = _attention_group(qkv3, cosf, sinf, perm, gi, dilation)
        outs.append(o)
        lses.append(lse)

    merged = _merge(y_ssd, outs, lses, p["w_ssd_br"], p["w_att_br"], gates, s)
    x1, x1_16 = _proj_ln(merged, p["w_o"], x, p["ln1_g"], p["ln1_b"])

    idx_t, wt_t, rank_t, counts = _router(x1, p["w_router_t"], p["router_bias"])
    block_e, n_active, row_tok, row_w, dest_t = _dispatch_plan(idx_t, wt_t, rank_t, counts)
    y_rows = _experts(x1, block_e, n_active, row_tok, row_w, w_gate_e, w_up_e, w_down_e, l)
    nt = t // COMBINE_TOKENS
    dest_blocks = dest_t.reshape(TOP_K, nt, COMBINE_TOKENS).transpose(1, 0, 2)
    dest_blocks = dest_blocks.reshape(nt, 1, TOP_K * COMBINE_TOKENS)
    return _combine(y_rows, dest_blocks, x1, x1_16, p["w_gate_s"], p["w_up_s"], p["w_down_s"],
                    p["ln2_g"], p["ln2_b"])


def _pad_heads(v):
    return jnp.pad(v.astype(F32), (0, HEAD_PAD - SSD_HEADS)).reshape(1, HEAD_PAD)


def _layer_params(l, conv_w, conv_b, dt_bias, a_log, d_skip, ssd_norm_w, w_ssd_br, w_att_br,
                  gate_b, w_o, ln1_g, ln1_b, w_router, router_bias, w_gate_s, w_up_s, w_down_s,
                  ln2_g, ln2_b):
    row = lambda v: v.astype(F32).reshape(1, -1)
    return dict(
        zero_z=jnp.zeros((1, D_INNER), F32), zero_xbc=jnp.zeros((1, CONV_DIM), F32),
        zero_qkv=jnp.zeros((1, 3 * ATT_WIDTH), F32),
        dt_bias=_pad_heads(dt_bias[l]), gate_b=row(gate_b[l]),
        conv_w=conv_w[l].astype(F32), conv_b=row(conv_b[l]),
        a_log=_pad_heads(a_log[l]), d_skip=_pad_heads(d_skip[l]), norm_w=row(ssd_norm_w[l]),
        w_ssd_br=w_ssd_br[l].astype(BF16), w_att_br=w_att_br[l].astype(BF16),
        w_o=w_o[l].astype(BF16), ln1_g=row(ln1_g[l]), ln1_b=row(ln1_b[l]),
        w_router_t=w_router[l].astype(F32).T, router_bias=router_bias[l].astype(F32).reshape(-1, 1),
        w_gate_s=w_gate_s[l].astype(BF16), w_up_s=w_up_s[l].astype(BF16),
        w_down_s=w_down_s[l].astype(BF16), ln2_g=row(ln2_g[l]), ln2_b=row(ln2_b[l]))


def _rope_tables(positions):
    half = ROPE_DIM // 2
    inv_freq = ROPE_THETA ** (-jnp.arange(0, ROPE_DIM, 2, dtype=F32) / ROPE_DIM)
    ang = positions.astype(F32)[..., None] * inv_freq
    cos, sin = jnp.cos(ang), jnp.sin(ang)
    pad = positions.shape + (ATT_HEAD_DIM - ROPE_DIM,)
    cosf = jnp.concatenate([cos, cos, jnp.ones(pad, F32)], axis=-1)
    sinf = jnp.concatenate([-sin, sin, jnp.zeros(pad, F32)], axis=-1)
    k = jnp.arange(ATT_HEAD_DIM)[:, None]
    j = jnp.arange(ATT_HEAD_DIM)[None, :]
    perm = ((j < half) & (k == j + half)) | ((j >= half) & (j < ROPE_DIM) & (k == j - half))
    return cosf, sinf, perm.astype(BF16)


def kernel(x, positions, w_in, conv_w, conv_b, dt_bias, a_log, d_skip, ssd_norm_w, w_ssd_br,
           w_att_br, gate_b, w_o, ln1_g, ln1_b, w_router, router_bias, w_gate_e, w_up_e, w_down_e,
           w_gate_s, w_up_s, w_down_s, ln2_g, ln2_b):
    bsz, s, d = x.shape
    cosf, sinf, perm = _rope_tables(positions)
    head_of_col = jnp.arange(D_INNER) // SSD_HEAD_DIM
    expand = (jnp.arange(HEAD_PAD)[:, None] == head_of_col[None, :]).astype(BF16)
    w_in = w_in.astype(F32)
    w_tail = w_in[:, :, TAIL_COL:]
    w_gate_e, w_up_e, w_down_e = (w.astype(F32) for w in (w_gate_e, w_up_e, w_down_e))
    xf = x.reshape(bsz * s, d).astype(F32)
    x16 = xf.astype(BF16)
    for l in range(DEPTH):
        p = _layer_params(l, conv_w, conv_b, dt_bias, a_log, d_skip, ssd_norm_w, w_ssd_br, w_att_br,
                          gate_b, w_o, ln1_g, ln1_b, w_router, router_bias, w_gate_s, w_up_s,
                          w_down_s, ln2_g, ln2_b)
        xf, x16 = _layer(xf, x16, cosf, sinf, perm, expand, bsz, s, l, w_in, w_tail,
                         w_gate_e, w_up_e, w_down_e, p)
    return xf.reshape(bsz, s, d).astype(x.dtype)
```

```python
import functools

import jax
import jax.numpy as jnp
from jax import lax
from jax.experimental import pallas as pl
from jax.experimental.pallas import tpu as pltpu

D_MODEL = 2048
DEPTH = 2

D_INNER = 4096
SSD_HEAD_DIM = 64
SSD_HEADS = 64
SSD_GROUPS = 8
SSD_HEADS_PER_GROUP = 8
SSD_STATE = 128
SSD_CONV = 4
SSD_CHUNK = 128
CONV_DIM = D_INNER + 2 * SSD_GROUPS * SSD_STATE
RMS_EPS = 1e-5

ATT_HEAD_DIM = 128
ATT_GROUPS = ((128, 1), (512, 4), (2048, 16))
ATT_HEADS_PER_GROUP = 4
ATT_WIDTH = 1536
ATT_OUT = 512
ATT_BLOCK = 128
ROPE_THETA = 500000.0
ROPE_DIM = 32

IN_SIZES = (D_INNER, CONV_DIM, SSD_HEADS, ATT_WIDTH, ATT_WIDTH, ATT_WIDTH, D_MODEL, D_MODEL)
DT_COL = D_INNER + CONV_DIM
TAIL_COL = DT_COL + SSD_HEADS

N_EXPERTS = 64
N_EXPERT_GROUPS = 8
EXPERTS_PER_GROUP = 8
TOPK_GROUPS = 4
TOP_K = 8
D_EXPERT = 512
D_SHARED = 512
ROUTED_SCALE = 2.5

DEEPNORM_ALPHA = (2 * DEPTH) ** 0.25
LN_EPS = 1e-5

LANES = 128
HEAD_PAD = 128
ROW_TILE = 8
MOE_ROWS = 512
COMBINE_TOKENS = 128
VMEM_LIMIT = 56 * 1024 * 1024

F32 = jnp.float32
BF16 = jnp.bfloat16


def _cparams(*sem):
    return pltpu.CompilerParams(dimension_semantics=sem, vmem_limit_bytes=VMEM_LIMIT)


def _sigmoid(v):
    return 1.0 / (1.0 + jnp.exp(-v))


def _mm_kernel(x_ref, w_ref, b_ref, o_ref, w16_ref, *, act):
    @pl.when(pl.program_id(1) == 0)
    def _():
        w16_ref[...] = w_ref[0].astype(BF16)

    acc = jnp.dot(x_ref[...], w16_ref[...], preferred_element_type=F32) + b_ref[...]
    if act == "silu":
        acc = acc * _sigmoid(acc)
    elif act == "sigmoid":
        acc = _sigmoid(acc)
    elif act == "softplus":
        acc = jnp.maximum(acc, 0.0) + jnp.log(1.0 + jnp.exp(-jnp.abs(acc)))
    o_ref[...] = acc.astype(o_ref.dtype)


def _matmul(x, w, layer, col0, n, bias, act, out_dtype, tm, tn):
    m, k = x.shape
    assert col0 % tn == 0 and n % tn == 0 and m % tm == 0
    j0 = col0 // tn
    return pl.pallas_call(
        functools.partial(_mm_kernel, act=act),
        out_shape=jax.ShapeDtypeStruct((m, n), out_dtype),
        grid=(n // tn, m // tm),
        in_specs=[pl.BlockSpec((tm, k), lambda j, i: (i, 0)),
                  pl.BlockSpec((1, k, tn), lambda j, i: (layer, 0, j0 + j)),
                  pl.BlockSpec((1, tn), lambda j, i: (0, j))],
        out_specs=pl.BlockSpec((tm, tn), lambda j, i: (i, j)),
        scratch_shapes=[pltpu.VMEM((k, tn), BF16)],
        compiler_params=_cparams("parallel", "arbitrary"),
        name="proj_" + act,
    )(x, w, bias)


def _split_dot(v, e):
    hi = v.astype(BF16)
    lo = (v - hi.astype(F32)).astype(BF16)
    return (jnp.dot(hi, e, preferred_element_type=F32)
            + jnp.dot(lo, e, preferred_element_type=F32))


def _split_dot_left(e, v):
    hi = v.astype(BF16)
    lo = (v - hi.astype(F32)).astype(BF16)
    return (jnp.dot(e, hi, preferred_element_type=F32)
            + jnp.dot(e, lo, preferred_element_type=F32))


def _ssd_kernel(xbc_ref, zs_ref, dt_ref, convw_ref, convb_ref, alog_ref, dskip_ref, normw_ref,
                expand_ref, o_ref, xpad_ref, state_ref, y_ref, dtx_ref, dtex_ref, eax_ref):
    q = SSD_CHUNK
    gw = SSD_HEADS_PER_GROUP * SSD_HEAD_DIM

    @pl.when(pl.program_id(1) == 0)
    def _():
        xpad_ref[0:8, :] = jnp.zeros((8, CONV_DIM), F32)
        state_ref[...] = jnp.zeros_like(state_ref)

    xpad_ref[8:8 + q, :] = xbc_ref[0].astype(F32)

    def conv(c0, width):
        acc = convb_ref[:, c0:c0 + width]
        for j in range(SSD_CONV):
            lo = 8 - (SSD_CONV - 1) + j
            acc = acc + convw_ref[j:j + 1, c0:c0 + width] * xpad_ref[lo:lo + q, c0:c0 + width]
        return acc * _sigmoid(acc)

    row = lax.broadcasted_iota(jnp.int32, (q, q), 0)
    col = lax.broadcasted_iota(jnp.int32, (q, q), 1)
    causal = col <= row
    tril = jnp.where(causal, 1.0, 0.0).astype(BF16)

    dt = dt_ref[0]
    a = -jnp.exp(alog_ref[...])
    da = dt * a
    a_cs = _split_dot_left(tril, da)
    a_cs_t = a_cs.T
    a_last = a_cs[q - 1:q, :]
    expand = expand_ref[...]
    dtx_ref[...] = _split_dot(dt, expand)
    dtex_ref[...] = _split_dot(dt * jnp.exp(a_last - a_cs), expand)
    eax_ref[...] = _split_dot(jnp.exp(a_cs), expand)
    decx = _split_dot(jnp.broadcast_to(jnp.exp(a_last), (8, HEAD_PAD)), expand)[0:1, :]
    dskipx = _split_dot(jnp.broadcast_to(dskip_ref[...], (8, HEAD_PAD)), expand)[0:1, :]

    for g in range(SSD_GROUPS):
        c0 = g * gw
        xs = conv(c0, gw)
        bm = conv(D_INNER + g * SSD_STATE, SSD_STATE)
        cm = conv(D_INNER + SSD_GROUPS * SSD_STATE + g * SSD_STATE, SSD_STATE)
        bm_t = bm.T.astype(BF16)
        cm16 = cm.astype(BF16)
        cb = jnp.dot(cm16, bm_t, preferred_element_type=F32)
        xdt = (xs * dtx_ref[:, c0:c0 + gw]).astype(BF16)
        for r in range(SSD_HEADS_PER_GROUP):
            h = g * SSD_HEADS_PER_GROUP + r
            seg = a_cs[:, h:h + 1] - a_cs_t[h:h + 1, :]
            decay = jnp.exp(jnp.where(causal, seg, -jnp.inf))
            m = (cb * decay).astype(BF16)
            y_ref[:, h * SSD_HEAD_DIM:(h + 1) * SSD_HEAD_DIM] = jnp.dot(
                m, xdt[:, r * SSD_HEAD_DIM:(r + 1) * SSD_HEAD_DIM], preferred_element_type=F32)
        st = state_ref[:, c0:c0 + gw]
        y_off = jnp.dot(cm16, st.astype(BF16), preferred_element_type=F32) * eax_ref[:, c0:c0 + gw]
        xdte = (xs * dtex_ref[:, c0:c0 + gw]).astype(BF16)
        state_ref[:, c0:c0 + gw] = decx[:, c0:c0 + gw] * st + jnp.dot(
            bm_t, xdte, preferred_element_type=F32)
        y = y_ref[:, c0:c0 + gw] + y_off + dskipx[:, c0:c0 + gw] * xs
        y = y * zs_ref[0, :, c0:c0 + gw].astype(F32)
        ms = jnp.mean(y * y, axis=-1, keepdims=True)
        o_ref[0, :, c0:c0 + gw] = (y * lax.rsqrt(ms + RMS_EPS)
                                   * normw_ref[:, c0:c0 + gw]).astype(o_ref.dtype)

    xpad_ref[0:8, :] = xpad_ref[q:q + 8, :]


def _ssd(xbc, zs, dt, conv_w, conv_b, a_log, d_skip, norm_w, expand):
    bsz, s, _ = xbc.shape
    nc = s // SSD_CHUNK
    full = lambda shape: pl.BlockSpec(shape, lambda b, c: (0, 0))
    return pl.pallas_call(
        _ssd_kernel,
        out_shape=jax.ShapeDtypeStruct((bsz, s, D_INNER), BF16),
        grid=(bsz, nc),
        in_specs=[pl.BlockSpec((1, SSD_CHUNK, CONV_DIM), lambda b, c: (b, c, 0)),
                  pl.BlockSpec((1, SSD_CHUNK, D_INNER), lambda b, c: (b, c, 0)),
                  pl.BlockSpec((1, SSD_CHUNK, HEAD_PAD), lambda b, c: (b, c, 0)),
                  full((SSD_CONV, CONV_DIM)), full((1, CONV_DIM)), full((1, HEAD_PAD)),
                  full((1, HEAD_PAD)), full((1, D_INNER)), full((HEAD_PAD, D_INNER))],
        out_specs=pl.BlockSpec((1, SSD_CHUNK, D_INNER), lambda b, c: (b, c, 0)),
        scratch_shapes=[pltpu.VMEM((SSD_CHUNK + 8, CONV_DIM), F32),
                        pltpu.VMEM((SSD_STATE, D_INNER), F32),
                        pltpu.VMEM((SSD_CHUNK, D_INNER), F32),
                        pltpu.VMEM((SSD_CHUNK, D_INNER), F32),
                        pltpu.VMEM((SSD_CHUNK, D_INNER), F32),
                        pltpu.VMEM((SSD_CHUNK, D_INNER), F32)],
        compiler_params=_cparams("parallel", "arbitrary"),
        name="ssd_mixer",
    )(xbc, zs, dt, conv_w, conv_b, a_log, d_skip, norm_w, expand)


ATT_STAGE_ROWS = 256


def _attn_kernel(q_ref, k_ref, v_ref, cos_ref, sin_ref, perm_ref, o_ref, lse_ref,
                 q32_ref, k32_ref, v32_ref, *, dilation):
    s = q_ref.shape[1]
    blk, dh, d = ATT_BLOCK, ATT_HEAD_DIM, dilation
    nb = s // d // blk
    perm = perm_ref[...]

    def stage(c, carry):
        rows = pl.ds(pl.multiple_of(c * ATT_STAGE_ROWS, ATT_STAGE_ROWS), ATT_STAGE_ROWS)
        cos, sin = cos_ref[0, rows, :], sin_ref[0, rows, :]
        for h in range(ATT_HEADS_PER_GROUP):
            sl = slice(h * dh, (h + 1) * dh)
            qh, kh = q_ref[0, rows, sl], k_ref[0, rows, sl]
            q_rot = qh.astype(F32) * cos + jnp.dot(qh, perm, preferred_element_type=F32) * sin
            q32_ref[h, rows, :] = q_rot * (dh ** -0.5)
            k32_ref[h, rows, :] = (kh.astype(F32) * cos
                                   + jnp.dot(kh, perm, preferred_element_type=F32) * sin)
            v32_ref[h, rows, :] = v_ref[0, rows, sl].astype(F32)
        return carry

    lax.fori_loop(0, s // ATT_STAGE_ROWS, stage, 0)

    qi = lax.broadcasted_iota(jnp.int32, (blk, blk), 0)
    kj = lax.broadcasted_iota(jnp.int32, (blk, blk), 1)
    mask_c = kj <= qi
    mask_p = kj >= qi
    qk = (((2,), (2,)), ((0,), (0,)))
    pv = (((2,), (1,)), ((0,), (0,)))

    units = [(r, n) for r in range(d) for n in range(nb)]

    def sub_rows(r, n):
        start = r + d * blk * n
        return pl.ds(start, blk, stride=d) if d > 1 else pl.ds(start, blk)

    def gather(ref, h, shift):
        return jnp.stack([ref[h, sub_rows(r, max(n - shift, 0)), :] for r, n in units]).astype(BF16)

    def head(h, carry):
        qb = gather(q32_ref, h, 0)
        kc, vc = gather(k32_ref, h, 0), gather(v32_ref, h, 0)
        s_c = jnp.where(mask_c[None], lax.dot_general(qb, kc, qk, preferred_element_type=F32),
                        -jnp.inf)
        m = jnp.max(s_c, axis=-1, keepdims=True)
        if nb > 1:
            kp, vp = gather(k32_ref, h, 1), gather(v32_ref, h, 1)
            has_prev = jnp.stack([jnp.full((blk, blk), n > 0) for _, n in units])
            s_p = lax.dot_general(qb, kp, qk, preferred_element_type=F32)
            s_p = jnp.where(has_prev, jnp.where(mask_p[None], s_p, -jnp.inf), -jnp.inf)
            m = jnp.maximum(m, jnp.max(s_p, axis=-1, keepdims=True))
        p_c = jnp.exp(s_c - m)
        den = jnp.sum(p_c, axis=-1, keepdims=True)
        acc = lax.dot_general(p_c.astype(BF16), vc, pv, preferred_element_type=F32)
        if nb > 1:
            p_p = jnp.exp(s_p - m)
            den = den + jnp.sum(p_p, axis=-1, keepdims=True)
            acc = acc + lax.dot_general(p_p.astype(BF16), vp, pv, preferred_element_type=F32)
        out = acc / den
        lse = jnp.broadcast_to(m + jnp.log(den), out.shape)
        for u, (r, n) in enumerate(units):
            o_ref[0, h, sub_rows(r, n), :] = out[u]
            lse_ref[0, h, sub_rows(r, n), :] = lse[u]
        return carry

    lax.fori_loop(0, ATT_HEADS_PER_GROUP, head, 0)


def _attention_group(qkv, cosf, sinf, perm, gi, dilation):
    bsz, s, _ = qkv.shape
    gw = ATT_HEADS_PER_GROUP * ATT_HEAD_DIM
    col = lambda which: pl.BlockSpec((1, s, gw), lambda b: (b, 0, which * 3 + gi))
    tab = pl.BlockSpec((1, s, LANES), lambda b: (b, 0, 0))
    out_spec = pl.BlockSpec((1, ATT_HEADS_PER_GROUP, s, ATT_HEAD_DIM), lambda b: (b, 0, 0, 0))
    shape = jax.ShapeDtypeStruct((bsz, ATT_HEADS_PER_GROUP, s, ATT_HEAD_DIM), F32)
    scratch = pltpu.VMEM((ATT_HEADS_PER_GROUP, s, ATT_HEAD_DIM), F32)
    return pl.pallas_call(
        functools.partial(_attn_kernel, dilation=dilation),
        out_shape=(shape, shape),
        grid=(bsz,),
        in_specs=[col(0), col(1), col(2), tab, tab, pl.BlockSpec((LANES, LANES), lambda b: (0, 0))],
        out_specs=(out_spec, out_spec),
        scratch_shapes=[scratch, scratch, scratch],
        compiler_params=_cparams("parallel"),
        name="dilated_attention_%d" % dilation,
    )(qkv, qkv, qkv, cosf, sinf, perm)


def _merge_kernel(y_ref, o1_ref, o2_ref, o3_ref, l1_ref, l2_ref, l3_ref, ws_ref, wa_ref,
                  gs_ref, ga_ref, out_ref, att_ref):
    @pl.when(pl.program_id(1) == 0)
    def _():
        for h in range(ATT_HEADS_PER_GROUP):
            l1, l2, l3 = l1_ref[0, h], l2_ref[0, h], l3_ref[0, h]
            m = jnp.maximum(jnp.maximum(l1, l2), l3)
            e1, e2, e3 = jnp.exp(l1 - m), jnp.exp(l2 - m), jnp.exp(l3 - m)
            num = e1 * o1_ref[0, h] + e2 * o2_ref[0, h] + e3 * o3_ref[0, h]
            att_ref[:, h * ATT_HEAD_DIM:(h + 1) * ATT_HEAD_DIM] = (num / (e1 + e2 + e3)).astype(BF16)

    y_ssd = jnp.dot(y_ref[...], ws_ref[...], preferred_element_type=F32)
    y_att = jnp.dot(att_ref[...], wa_ref[...], preferred_element_type=F32)
    out_ref[...] = (gs_ref[...].astype(F32) * y_ssd
                    + ga_ref[...].astype(F32) * y_att).astype(out_ref.dtype)


def _merge(y_ssd, outs, lses, w_ssd_br, w_att_br, gates, s, tm=512, tn=512):
    t = y_ssd.shape[0]
    nj = D_MODEL // tn
    per_batch = s // tm
    head_major = pl.BlockSpec((1, ATT_HEADS_PER_GROUP, tm, ATT_HEAD_DIM),
                              lambda i, j: (i // per_batch, 0, i % per_batch, 0))
    return pl.pallas_call(
        _merge_kernel,
        out_shape=jax.ShapeDtypeStruct((t, D_MODEL), BF16),
        grid=(t // tm, nj),
        in_specs=[pl.BlockSpec((tm, D_INNER), lambda i, j: (i, 0))] + [head_major] * 6
                 + [pl.BlockSpec((D_INNER, tn), lambda i, j: (0, j)),
                    pl.BlockSpec((ATT_OUT, tn), lambda i, j: (0, j)),
                    pl.BlockSpec((tm, tn), lambda i, j: (i, j)),
                    pl.BlockSpec((tm, tn), lambda i, j: (i, j + nj))],
        out_specs=pl.BlockSpec((tm, tn), lambda i, j: (i, j)),
        scratch_shapes=[pltpu.VMEM((tm, ATT_OUT), BF16)],
        compiler_params=_cparams("parallel", "arbitrary"),
        name="branch_merge",
    )(y_ssd, *outs, *lses, w_ssd_br, w_att_br, gates, gates)


def _layer_norm(v, g, b):
    mu = jnp.mean(v, axis=-1, keepdims=True)
    c = v - mu
    var = jnp.mean(c * c, axis=-1, keepdims=True)
    return c * lax.rsqrt(var + LN_EPS) * g + b


def _pack_rows(v):
    half = D_MODEL // 2

    def bits(a):
        return lax.bitcast_convert_type(a.astype(BF16).astype(F32), jnp.uint32)

    return [(bits(v[:, half + s * LANES:half + (s + 1) * LANES]) & jnp.uint32(0xFFFF0000))
            | (bits(v[:, s * LANES:(s + 1) * LANES]) >> 16) for s in range(ROW_TILE)]


def _unpack_pair(c):
    return (lax.bitcast_convert_type(c << 16, F32),
            lax.bitcast_convert_type(c & jnp.uint32(0xFFFF0000), F32))


def _unpack_rows(chunks):
    pairs = [_unpack_pair(c) for c in chunks]
    return (jnp.concatenate([p[0] for p in pairs], axis=1),
            jnp.concatenate([p[1] for p in pairs], axis=1))


def _load_token_tiles(ref, row0, n_tokens):
    return [ref[pl.ds(row0 + s, n_tokens, stride=ROW_TILE), :] for s in range(ROW_TILE)]


def _store_token_tiles(ref, chunks):
    n_tokens = chunks[0].shape[0]
    for s, chunk in enumerate(chunks):
        ref[pl.ds(s, n_tokens, stride=ROW_TILE), :] = chunk


def _proj_ln_kernel(m_ref, w_ref, x_ref, g_ref, b_ref, o_ref, op_ref):
    v = DEEPNORM_ALPHA * x_ref[...] + jnp.dot(m_ref[...], w_ref[...], preferred_element_type=F32)
    out = _layer_norm(v, g_ref[...], b_ref[...])
    o_ref[...] = out
    _store_token_tiles(op_ref, _pack_rows(out))


def _proj_ln(merged, w_o, x, g, b, tm=256):
    t = x.shape[0]
    row = lambda: pl.BlockSpec((tm, D_MODEL), lambda i: (i, 0))
    const = lambda shape: pl.BlockSpec(shape, lambda i: (0, 0))
    return pl.pallas_call(
        _proj_ln_kernel,
        out_shape=(jax.ShapeDtypeStruct((t, D_MODEL), F32),
                   jax.ShapeDtypeStruct((t * ROW_TILE, LANES), jnp.uint32)),
        grid=(t // tm,),
        in_specs=[row(), const((D_MODEL, D_MODEL)), row(), const((1, D_MODEL)), const((1, D_MODEL))],
        out_specs=(row(), pl.BlockSpec((tm * ROW_TILE, LANES), lambda i: (i, 0))),
        compiler_params=_cparams("parallel"),
        name="out_proj_layernorm",
    )(merged, w_o, x, g, b)


def _router_kernel(x_ref, w_ref, bias_ref, idx_ref, wt_ref, rank_ref, cnt_ref, carry_ref):
    tm = x_ref.shape[0]
    ne, ng, eg = N_EXPERTS, N_EXPERT_GROUPS, EXPERTS_PER_GROUP

    @pl.when(pl.program_id(0) == 0)
    def _():
        carry_ref[...] = jnp.zeros_like(carry_ref)

    logits = lax.dot_general(w_ref[...], x_ref[...], (((1,), (1,)), ((), ())),
                             precision=lax.Precision.HIGHEST, preferred_element_type=F32)
    scores = _sigmoid(logits)
    sel = scores + bias_ref[...]
    sub = lax.broadcasted_iota(jnp.int32, (eg, tm), 0)
    neg = -jnp.inf

    gscore = []
    for g in range(ng):
        sg = sel[g * eg:(g + 1) * eg, :]
        m1 = jnp.max(sg, axis=0, keepdims=True)
        first = jnp.min(jnp.where(sg == m1, sub, eg), axis=0, keepdims=True)
        m2 = jnp.max(jnp.where(sub == first, neg, sg), axis=0, keepdims=True)
        gscore.append(m1 + m2)
    keep = []
    for g in range(ng):
        beaten = jnp.zeros((1, tm), jnp.int32)
        for o in range(ng):
            if o == g:
                continue
            wins = (gscore[o] >= gscore[g]) if o < g else (gscore[o] > gscore[g])
            beaten = beaten + wins.astype(jnp.int32)
        keep.append(beaten < TOPK_GROUPS)
    masked = jnp.concatenate(
        [jnp.where(keep[g], sel[g * eg:(g + 1) * eg, :], neg) for g in range(ng)], axis=0)

    eidx = lax.broadcasted_iota(jnp.int32, (ne, tm), 0)
    picks, weights, chosen_masks = [], [], []
    for _ in range(TOP_K):
        mx = jnp.max(masked, axis=0, keepdims=True)
        pick = jnp.min(jnp.where(masked == mx, eidx, ne), axis=0, keepdims=True)
        chosen = eidx == pick
        weights.append(jnp.sum(jnp.where(chosen, scores, 0.0), axis=0, keepdims=True))
        picks.append(pick)
        chosen_masks.append(chosen)
        masked = jnp.where(chosen, neg, masked)
    w = jnp.concatenate(weights, axis=0)
    idx_ref[...] = jnp.concatenate(picks, axis=0)
    wt_ref[...] = w / jnp.sum(w, axis=0, keepdims=True) * ROUTED_SCALE

    cnt = jnp.zeros((ne, tm), F32)
    for chosen in chosen_masks:
        cnt = cnt + jnp.where(chosen, 1.0, 0.0)
    earlier = (lax.broadcasted_iota(jnp.int32, (tm, tm), 0)
               < lax.broadcasted_iota(jnp.int32, (tm, tm), 1))
    before = jnp.dot(cnt.astype(BF16), jnp.where(earlier, 1.0, 0.0).astype(BF16),
                     preferred_element_type=F32) + carry_ref[...]
    rank_ref[...] = jnp.concatenate(
        [jnp.sum(jnp.where(chosen, before, 0.0), axis=0, keepdims=True) for chosen in chosen_masks],
        axis=0).astype(jnp.int32)
    carry_ref[...] = carry_ref[...] + jnp.sum(cnt, axis=1, keepdims=True)
    cnt_ref[...] = carry_ref[...]


def _router(x, w_router_t, bias, tm=512):
    t = x.shape[0]
    tok = lambda: pl.BlockSpec((TOP_K, tm), lambda i: (0, i))
    return pl.pallas_call(
        _router_kernel,
        out_shape=(jax.ShapeDtypeStruct((TOP_K, t), jnp.int32),
                   jax.ShapeDtypeStruct((TOP_K, t), F32),
                   jax.ShapeDtypeStruct((TOP_K, t), jnp.int32),
                   jax.ShapeDtypeStruct((N_EXPERTS, 1), F32)),
        grid=(t // tm,),
        in_specs=[pl.BlockSpec((tm, D_MODEL), lambda i: (i, 0)),
                  pl.BlockSpec((N_EXPERTS, D_MODEL), lambda i: (0, 0)),
                  pl.BlockSpec((N_EXPERTS, 1), lambda i: (0, 0))],
        out_specs=(tok(), tok(), tok(), pl.BlockSpec((N_EXPERTS, 1), lambda i: (0, 0))),
        scratch_shapes=[pltpu.VMEM((N_EXPERTS, 1), F32)],
        compiler_params=_cparams("arbitrary"),
        name="router_topk",
    )(x, w_router_t, bias)


def _gather_tiles(src_hbm, idx_ref, dst_ref, sem, r0, r1):
    for r in range(r0, r1):
        first = pl.multiple_of(idx_ref[0, 0, r], ROW_TILE)
        pltpu.make_async_copy(src_hbm.at[pl.ds(first, ROW_TILE), :],
                              dst_ref.at[pl.ds(r * ROW_TILE, ROW_TILE), :], sem).start()


def _wait_tiles(src_hbm, dst_ref, sem, n_tokens):
    pltpu.make_async_copy(src_hbm.at[pl.ds(0, n_tokens * ROW_TILE), :], dst_ref, sem).wait()


def _experts_kernel(be_ref, na_ref, tok0_ref, tokn_ref, x_hbm, wrow_ref, wg_ref, wu_ref, wd_ref,
                    y_ref, buf_ref, sem_ref, wg16_ref, wu16_ref, wd16_ref):
    i = pl.program_id(0)
    n_active = na_ref[0]
    slot = lax.rem(i, 2)
    half = D_MODEL // 2

    @pl.when((i == 0) & (n_active > 0))
    def _():
        _gather_tiles(x_hbm, tok0_ref, buf_ref.at[0], sem_ref.at[0], 0, MOE_ROWS)

    new_expert = (i == 0) | (be_ref[i] != be_ref[jnp.maximum(i - 1, 0)])

    @pl.when(new_expert & (i < n_active))
    def _():
        wg16_ref[...] = wg_ref[0].astype(BF16)
        wu16_ref[...] = wu_ref[0].astype(BF16)
        wd16_ref[...] = wd_ref[0].astype(BF16)

    @pl.when(i < n_active)
    def _():
        cur, nxt = buf_ref.at[slot], buf_ref.at[1 - slot]
        _wait_tiles(x_hbm, cur, sem_ref.at[slot], MOE_ROWS)
        group = MOE_ROWS // ROW_TILE
        gate = up = None
        for pair in range(ROW_TILE // 2):
            lo, hi = [], []
            for s in (2 * pair, 2 * pair + 1):
                _gather_tiles(x_hbm, tokn_ref, nxt, sem_ref.at[1 - slot], s * group, (s + 1) * group)
                c_lo, c_hi = _unpack_pair(cur[pl.ds(s, MOE_ROWS, stride=ROW_TILE), :])
                lo.append(c_lo.astype(BF16))
                hi.append(c_hi.astype(BF16))
            x_lo, x_hi = jnp.concatenate(lo, axis=1), jnp.concatenate(hi, axis=1)
            k_lo = slice(2 * pair * LANES, (2 * pair + 2) * LANES)
            k_hi = slice(half + 2 * pair * LANES, half + (2 * pair + 2) * LANES)
            g = (jnp.dot(x_lo, wg16_ref[k_lo, :], preferred_element_type=F32)
                 + jnp.dot(x_hi, wg16_ref[k_hi, :], preferred_element_type=F32))
            u = (jnp.dot(x_lo, wu16_ref[k_lo, :], preferred_element_type=F32)
                 + jnp.dot(x_hi, wu16_ref[k_hi, :], preferred_element_type=F32))
            gate = g if gate is None else gate + g
            up = u if up is None else up + u
        h = gate * _sigmoid(gate) * up * wrow_ref[...]
        y = jnp.dot(h.astype(BF16), wd16_ref[...], preferred_element_type=F32)
        _store_token_tiles(y_ref, _pack_rows(y))

    @pl.when(i == n_active - 1)
    def _():
        _wait_tiles(x_hbm, buf_ref.at[1 - slot], sem_ref.at[1 - slot], MOE_ROWS)

    @pl.when(i >= n_active)
    def _():
        y_ref[...] = jnp.zeros_like(y_ref)


def _experts(xp, block_e, n_active, row_tok, row_w, w_gate, w_up, w_down, layer):
    n_blocks = row_tok.shape[0]
    n_rows = n_blocks * MOE_ROWS
    smem_blk = lambda index_map: pl.BlockSpec((1, 1, MOE_ROWS), index_map, memory_space=pltpu.SMEM)
    wspec = lambda a, b: pl.BlockSpec((1, 1, a, b), lambda i, be, na: (layer, be[i], 0, 0))

    def kernel(be_ref, na_ref, tok0_ref, tokn_ref, x_hbm, wrow_ref, wg_ref, wu_ref, wd_ref, *rest):
        _experts_kernel(be_ref, na_ref, tok0_ref, tokn_ref, x_hbm, wrow_ref, wg_ref.at[0],
                        wu_ref.at[0], wd_ref.at[0], *rest)

    grid_spec = pltpu.PrefetchScalarGridSpec(
        num_scalar_prefetch=2,
        grid=(n_blocks,),
        in_specs=[smem_blk(lambda i, be, na: (0, 0, 0)),
                  smem_blk(lambda i, be, na: (jnp.minimum(i + 1, n_blocks - 1), 0, 0)),
                  pl.BlockSpec(memory_space=pl.ANY),
                  pl.BlockSpec((MOE_ROWS, 1), lambda i, be, na: (i, 0)),
                  wspec(D_MODEL, D_EXPERT), wspec(D_MODEL, D_EXPERT), wspec(D_EXPERT, D_MODEL)],
        out_specs=pl.BlockSpec((MOE_ROWS * ROW_TILE, LANES), lambda i, be, na: (i, 0)),
        scratch_shapes=[pltpu.VMEM((2, MOE_ROWS * ROW_TILE, LANES), jnp.uint32),
                        pltpu.SemaphoreType.DMA((2,)),
                        pltpu.VMEM((D_MODEL, D_EXPERT), BF16),
                        pltpu.VMEM((D_MODEL, D_EXPERT), BF16),
                        pltpu.VMEM((D_EXPERT, D_MODEL), BF16)])
    return pl.pallas_call(
        kernel,
        out_shape=jax.ShapeDtypeStruct((n_rows * ROW_TILE, LANES), jnp.uint32),
        grid_spec=grid_spec,
        compiler_params=_cparams("arbitrary"),
        name="routed_experts",
    )(block_e, n_active, row_tok, row_tok, xp, row_w, w_gate, w_up, w_down)


def _combine_kernel(d0_ref, dn_ref, y_hbm, x_ref, wg_ref, wu_ref, wd_ref, g_ref, b_ref,
                    o_ref, o16_ref, buf_ref, sem_ref):
    i = pl.program_id(0)
    n = pl.num_programs(0)
    slot = lax.rem(i, 2)
    tm = COMBINE_TOKENS
    rows = tm * TOP_K

    @pl.when(i == 0)
    def _():
        _gather_tiles(y_hbm, d0_ref, buf_ref.at[0], sem_ref.at[0], 0, rows)

    _wait_tiles(y_hbm, buf_ref.at[slot], sem_ref.at[slot], rows)
    _gather_tiles(y_hbm, dn_ref, buf_ref.at[1 - slot], sem_ref.at[1 - slot], 0, rows)

    x = x_ref[...]
    x16 = x.astype(BF16)
    gate = jnp.dot(x16, wg_ref[...], preferred_element_type=F32)
    up = jnp.dot(x16, wu_ref[...], preferred_element_type=F32)
    h = (gate * _sigmoid(gate) * up).astype(BF16)
    v = DEEPNORM_ALPHA * x + jnp.dot(h, wd_ref[...], preferred_element_type=F32)

    lo = [None] * ROW_TILE
    hi = [None] * ROW_TILE
    for k in range(TOP_K):
        chunks = _load_token_tiles(buf_ref.at[slot], k * tm * ROW_TILE, tm)
        for s, c in enumerate(chunks):
            c_lo, c_hi = _unpack_pair(c)
            lo[s] = c_lo if k == 0 else lo[s] + c_lo
            hi[s] = c_hi if k == 0 else hi[s] + c_hi
    v = v + jnp.concatenate(lo + hi, axis=1)
    out = _layer_norm(v, g_ref[...], b_ref[...])
    o_ref[...] = out
    o16_ref[...] = out.astype(BF16)

    @pl.when(i == n - 1)
    def _():
        _wait_tiles(y_hbm, buf_ref.at[1 - slot], sem_ref.at[1 - slot], rows)


def _combine(y_rows, dest_blocks, x, w_gate_s, w_up_s, w_down_s, g, b):
    t = x.shape[0]
    tm = COMBINE_TOKENS
    rows = tm * TOP_K
    row = lambda: pl.BlockSpec((tm, D_MODEL), lambda i: (i, 0))
    const = lambda shape: pl.BlockSpec(shape, lambda i: (0, 0))
    nblk = t // tm
    smem_blk = lambda index_map: pl.BlockSpec((1, 1, rows), index_map, memory_space=pltpu.SMEM)
    return pl.pallas_call(
        _combine_kernel,
        out_shape=(jax.ShapeDtypeStruct((t, D_MODEL), F32), jax.ShapeDtypeStruct((t, D_MODEL), BF16)),
        grid=(nblk,),
        in_specs=[smem_blk(lambda i: (0, 0, 0)),
                  smem_blk(lambda i: (jnp.minimum(i + 1, nblk - 1), 0, 0)),
                  pl.BlockSpec(memory_space=pl.ANY),
                  row(),
                  const((D_MODEL, D_SHARED)), const((D_MODEL, D_SHARED)), const((D_SHARED, D_MODEL)),
                  const((1, D_MODEL)), const((1, D_MODEL))],
        out_specs=(row(), row()),
        scratch_shapes=[pltpu.VMEM((2, rows * ROW_TILE, LANES), jnp.uint32),
                        pltpu.SemaphoreType.DMA((2,))],
        compiler_params=_cparams("arbitrary"),
        name="moe_combine_layernorm",
    )(dest_blocks, dest_blocks, y_rows, x, w_gate_s, w_up_s, w_down_s, g, b)


def _dispatch_plan(idx_t, wt_t, rank_t, counts):
    t = idx_t.shape[1]
    tk = t * TOP_K
    n_blocks = tk // MOE_ROWS + N_EXPERTS
    experts = jnp.arange(N_EXPERTS, dtype=jnp.int32)
    counts = counts.reshape(N_EXPERTS).astype(jnp.int32)
    padded = (counts + MOE_ROWS - 1) // MOE_ROWS * MOE_ROWS
    pad_end = jnp.cumsum(padded)
    pad_start = pad_end - padded
    start = jnp.cumsum(counts) - counts
    first_row = jnp.sum(jnp.where(idx_t[None] == experts[:, None, None],
                                  pad_start[:, None, None], 0), axis=0)
    dest_t = (first_row + rank_t).astype(jnp.int32)
    tok = jnp.broadcast_to(jnp.arange(t, dtype=jnp.int32)[None, :], (TOP_K, t))
    _, sorted_tok, sorted_w = lax.sort((dest_t.reshape(-1), tok.reshape(-1), wt_t.reshape(-1)),
                                       num_keys=1)
    block_first = jnp.arange(n_blocks, dtype=jnp.int32) * MOE_ROWS
    block_e = jnp.minimum(jnp.sum(pad_end[None, :] <= block_first[:, None], axis=1),
                          N_EXPERTS - 1).astype(jnp.int32)
    n_active = (pad_end[-1] // MOE_ROWS).astype(jnp.int32).reshape(1)
    within = (block_first - pad_start[block_e])[:, None] + jnp.arange(MOE_ROWS, dtype=jnp.int32)[None, :]
    valid = (within < counts[block_e][:, None]) & (block_first < pad_end[-1])[:, None]
    pos = jnp.clip(start[block_e][:, None] + within, 0, tk - 1)
    row_tok = (jnp.where(valid, sorted_tok[pos], 0) * ROW_TILE).astype(jnp.int32)
    row_w = jnp.where(valid, sorted_w[pos], 0.0).astype(F32)
    return (block_e, n_active, row_tok.reshape(n_blocks, 1, MOE_ROWS),
            row_w.reshape(n_blocks * MOE_ROWS, 1), dest_t)


def _layer(x, x16, cosf, sinf, perm, expand, bsz, s, l, w_in, w_tail, w_gate_e, w_up_e, w_down_e, p):
    t = bsz * s
    zs = _matmul(x16, w_in, l, 0, D_INNER, p["zero_z"], "silu", BF16, 1024, 1024)
    xbc = _matmul(x16, w_in, l, D_INNER, CONV_DIM, p["zero_xbc"], "none", BF16, 1024, 1024)
    dt = _matmul(x16, w_in, l, DT_COL, HEAD_PAD, p["dt_bias"], "softplus", F32, 1024, HEAD_PAD)
    qkv = _matmul(x16, w_tail, l, 0, 3 * ATT_WIDTH, p["zero_qkv"], "none", BF16, 1024, 768)
    gates = _matmul(x16, w_tail, l, 3 * ATT_WIDTH, 2 * D_MODEL, p["gate_b"], "sigmoid", BF16, 1024, 512)

    y_ssd = _ssd(xbc.reshape(bsz, s, CONV_DIM), zs.reshape(bsz, s, D_INNER),
                 dt.reshape(bsz, s, HEAD_PAD), p["conv_w"], p["conv_b"], p["a_log"], p["d_skip"],
                 p["norm_w"], expand).reshape(t, D_INNER)

    qkv3 = qkv.reshape(bsz, s, 3 * ATT_WIDTH)
    outs, lses = [], []
    for gi, (_, dilation) in enumerate(ATT_GROUPS):
        o, lse = _attention_group(qkv3, cosf, sinf, perm, gi, dilation)
        outs.append(o)
        lses.append(lse)

    merged = _merge(y_ssd, outs, lses, p["w_ssd_br"], p["w_att_br"], gates, s)
    x1, x1p = _proj_ln(merged, p["w_o"], x, p["ln1_g"], p["ln1_b"])

    idx_t, wt_t, rank_t, counts = _router(x1, p["w_router_t"], p["router_bias"])
    block_e, n_active, row_tok, row_w, dest_t = _dispatch_plan(idx_t, wt_t, rank_t, counts)
    y_rows = _experts(x1p, block_e, n_active, row_tok, row_w, w_gate_e, w_up_e, w_down_e, l)
    nt = t // COMBINE_TOKENS
    dest_blocks = (dest_t * ROW_TILE).reshape(TOP_K, nt, COMBINE_TOKENS).transpose(1, 0, 2)
    dest_blocks = dest_blocks.reshape(nt, 1, TOP_K * COMBINE_TOKENS)
    return _combine(y_rows, dest_blocks, x1, p["w_gate_s"], p["w_up_s"], p["w_down_s"],
                    p["ln2_g"], p["ln2_b"])


def _pad_heads(v):
    return jnp.pad(v.astype(F32), (0, HEAD_PAD - SSD_HEADS)).reshape(1, HEAD_PAD)


def _layer_params(l, conv_w, conv_b, dt_bias, a_log, d_skip, ssd_norm_w, w_ssd_br, w_att_br,
                  gate_b, w_o, ln1_g, ln1_b, w_router, router_bias, w_gate_s, w_up_s, w_down_s,
                  ln2_g, ln2_b):
    row = lambda v: v.astype(F32).reshape(1, -1)
    return dict(
        zero_z=jnp.zeros((1, D_INNER), F32), zero_xbc=jnp.zeros((1, CONV_DIM), F32),
        zero_qkv=jnp.zeros((1, 3 * ATT_WIDTH), F32),
        dt_bias=_pad_heads(dt_bias[l]), gate_b=row(gate_b[l]),
        conv_w=conv_w[l].astype(F32), conv_b=row(conv_b[l]),
        a_log=_pad_heads(a_log[l]), d_skip=_pad_heads(d_skip[l]), norm_w=row(ssd_norm_w[l]),
        w_ssd_br=w_ssd_br[l].astype(BF16), w_att_br=w_att_br[l].astype(BF16),
        w_o=w_o[l].astype(BF16), ln1_g=row(ln1_g[l]), ln1_b=row(ln1_b[l]),
        w_router_t=w_router[l].astype(F32).T, router_bias=router_bias[l].astype(F32).reshape(-1, 1),
        w_gate_s=w_gate_s[l].astype(BF16), w_up_s=w_up_s[l].astype(BF16),
        w_down_s=w_down_s[l].astype(BF16), ln2_g=row(ln2_g[l]), ln2_b=row(ln2_b[l]))


def _rope_tables(positions):
    half = ROPE_DIM // 2
    inv_freq = ROPE_THETA ** (-jnp.arange(0, ROPE_DIM, 2, dtype=F32) / ROPE_DIM)
    ang = positions.astype(F32)[..., None] * inv_freq
    cos, sin = jnp.cos(ang), jnp.sin(ang)
    pad = positions.shape + (ATT_HEAD_DIM - ROPE_DIM,)
    cosf = jnp.concatenate([cos, cos, jnp.ones(pad, F32)], axis=-1)
    sinf = jnp.concatenate([-sin, sin, jnp.zeros(pad, F32)], axis=-1)
    k = jnp.arange(ATT_HEAD_DIM)[:, None]
    j = jnp.arange(ATT_HEAD_DIM)[None, :]
    perm = ((j < half) & (k == j + half)) | ((j >= half) & (j < ROPE_DIM) & (k == j - half))
    return cosf, sinf, perm.astype(BF16)


def kernel(x, positions, w_in, conv_w, conv_b, dt_bias, a_log, d_skip, ssd_norm_w, w_ssd_br,
           w_att_br, gate_b, w_o, ln1_g, ln1_b, w_router, router_bias, w_gate_e, w_up_e, w_down_e,
           w_gate_s, w_up_s, w_down_s, ln2_g, ln2_b):
    bsz, s, d = x.shape
    cosf, sinf, perm = _rope_tables(positions)
    head_of_col = jnp.arange(D_INNER) // SSD_HEAD_DIM
    expand = (jnp.arange(HEAD_PAD)[:, None] == head_of_col[None, :]).astype(BF16)
    w_in = w_in.astype(F32)
    w_tail = w_in[:, :, TAIL_COL:]
    w_gate_e, w_up_e, w_down_e = (w.astype(F32) for w in (w_gate_e, w_up_e, w_down_e))
    xf = x.reshape(bsz * s, d).astype(F32)
    x16 = xf.astype(BF16)
    for l in range(DEPTH):
        p = _layer_params(l, conv_w, conv_b, dt_bias, a_log, d_skip, ssd_norm_w, w_ssd_br, w_att_br,
                          gate_b, w_o, ln1_g, ln1_b, w_router, router_bias, w_gate_s, w_up_s,
                          w_down_s, ln2_g, ln2_b)
        xf, x16 = _layer(xf, x16, cosf, sinf, perm, expand, bsz, s, l, w_in, w_tail,
                         w_gate_e, w_up_e, w_down_e, p)
    return xf.reshape(bsz, s, d).astype(x.dtype)
```

```python
import functools

import jax
import jax.numpy as jnp
from jax import lax
from jax.experimental import pallas as pl
from jax.experimental.pallas import tpu as pltpu

D_MODEL = 2048
DEPTH = 2

D_INNER = 4096
SSD_HEAD_DIM = 64
SSD_HEADS = 64
SSD_GROUPS = 8
SSD_HEADS_PER_GROUP = 8
SSD_STATE = 128
SSD_CONV = 4
SSD_CHUNK = 128
CONV_DIM = D_INNER + 2 * SSD_GROUPS * SSD_STATE
RMS_EPS = 1e-5

ATT_HEAD_DIM = 128
ATT_GROUPS = ((128, 1), (512, 4), (2048, 16))
ATT_HEADS_PER_GROUP = 4
ATT_WIDTH = 1536
ATT_OUT = 512
ATT_BLOCK = 128
ROPE_THETA = 500000.0
ROPE_DIM = 32

IN_SIZES = (D_INNER, CONV_DIM, SSD_HEADS, ATT_WIDTH, ATT_WIDTH, ATT_WIDTH, D_MODEL, D_MODEL)
DT_COL = D_INNER + CONV_DIM
TAIL_COL = DT_COL + SSD_HEADS

N_EXPERTS = 64
N_EXPERT_GROUPS = 8
EXPERTS_PER_GROUP = 8
TOPK_GROUPS = 4
TOP_K = 8
D_EXPERT = 512
D_SHARED = 512
ROUTED_SCALE = 2.5

DEEPNORM_ALPHA = (2 * DEPTH) ** 0.25
LN_EPS = 1e-5

LANES = 128
HEAD_PAD = 128
ROW_TILE = 8
MOE_ROWS = 512
COMBINE_TOKENS = 128
VMEM_LIMIT = 56 * 1024 * 1024

F32 = jnp.float32
BF16 = jnp.bfloat16


def _cparams(*sem):
    return pltpu.CompilerParams(dimension_semantics=sem, vmem_limit_bytes=VMEM_LIMIT)


def _sigmoid(v):
    return 1.0 / (1.0 + jnp.exp(-v))


def _mm_kernel(x_ref, w_ref, b_ref, o_ref, w16_ref, *, act):
    @pl.when(pl.program_id(1) == 0)
    def _():
        w16_ref[...] = w_ref[0].astype(BF16)

    acc = jnp.dot(x_ref[...], w16_ref[...], preferred_element_type=F32) + b_ref[...]
    if act == "silu":
        acc = acc * _sigmoid(acc)
    elif act == "sigmoid":
        acc = _sigmoid(acc)
    elif act == "softplus":
        acc = jnp.maximum(acc, 0.0) + jnp.log(1.0 + jnp.exp(-jnp.abs(acc)))
    o_ref[...] = acc.astype(o_ref.dtype)


def _matmul(x, w, layer, col0, n, bias, act, out_dtype, tm, tn):
    m, k = x.shape
    assert col0 % tn == 0 and n % tn == 0 and m % tm == 0
    j0 = col0 // tn
    return pl.pallas_call(
        functools.partial(_mm_kernel, act=act),
        out_shape=jax.ShapeDtypeStruct((m, n), out_dtype),
        grid=(n // tn, m // tm),
        in_specs=[pl.BlockSpec((tm, k), lambda j, i: (i, 0)),
                  pl.BlockSpec((1, k, tn), lambda j, i: (layer, 0, j0 + j)),
                  pl.BlockSpec((1, tn), lambda j, i: (0, j))],
        out_specs=pl.BlockSpec((tm, tn), lambda j, i: (i, j)),
        scratch_shapes=[pltpu.VMEM((k, tn), BF16)],
        compiler_params=_cparams("parallel", "arbitrary"),
        name="proj_" + act,
    )(x, w, bias)


def _split_dot(v, e):
    hi = v.astype(BF16)
    lo = (v - hi.astype(F32)).astype(BF16)
    return (jnp.dot(hi, e, preferred_element_type=F32)
            + jnp.dot(lo, e, preferred_element_type=F32))


def _split_dot_left(e, v):
    hi = v.astype(BF16)
    lo = (v - hi.astype(F32)).astype(BF16)
    return (jnp.dot(e, hi, preferred_element_type=F32)
            + jnp.dot(e, lo, preferred_element_type=F32))


def _ssd_kernel(xbc_ref, zs_ref, dt_ref, convw_ref, convb_ref, alog_ref, dskip_ref, normw_ref,
                expand_ref, o_ref, xpad_ref, state_ref, y_ref, dtx_ref, dtex_ref, eax_ref):
    q = SSD_CHUNK
    gw = SSD_HEADS_PER_GROUP * SSD_HEAD_DIM

    @pl.when(pl.program_id(1) == 0)
    def _():
        xpad_ref[0:8, :] = jnp.zeros((8, CONV_DIM), F32)
        state_ref[...] = jnp.zeros_like(state_ref)

    xpad_ref[8:8 + q, :] = xbc_ref[0].astype(F32)

    def conv(c0, width):
        acc = convb_ref[:, c0:c0 + width]
        for j in range(SSD_CONV):
            lo = 8 - (SSD_CONV - 1) + j
            acc = acc + convw_ref[j:j + 1, c0:c0 + width] * xpad_ref[lo:lo + q, c0:c0 + width]
        return acc * _sigmoid(acc)

    row = lax.broadcasted_iota(jnp.int32, (q, q), 0)
    col = lax.broadcasted_iota(jnp.int32, (q, q), 1)
    causal = col <= row
    tril = jnp.where(causal, 1.0, 0.0).astype(BF16)

    dt = dt_ref[0]
    a = -jnp.exp(alog_ref[...])
    da = dt * a
    a_cs = _split_dot_left(tril, da)
    a_cs_t = a_cs.T
    a_last = a_cs[q - 1:q, :]
    expand = expand_ref[...]
    dtx_ref[...] = _split_dot(dt, expand)
    dtex_ref[...] = _split_dot(dt * jnp.exp(a_last - a_cs), expand)
    eax_ref[...] = _split_dot(jnp.exp(a_cs), expand)
    decx = _split_dot(jnp.broadcast_to(jnp.exp(a_last), (8, HEAD_PAD)), expand)[0:1, :]
    dskipx = _split_dot(jnp.broadcast_to(dskip_ref[...], (8, HEAD_PAD)), expand)[0:1, :]

    for g in range(SSD_GROUPS):
        c0 = g * gw
        xs = conv(c0, gw)
        bm = conv(D_INNER + g * SSD_STATE, SSD_STATE)
        cm = conv(D_INNER + SSD_GROUPS * SSD_STATE + g * SSD_STATE, SSD_STATE)
        bm_t = bm.T.astype(BF16)
        cm16 = cm.astype(BF16)
        cb = jnp.dot(cm16, bm_t, preferred_element_type=F32)
        xdt = (xs * dtx_ref[:, c0:c0 + gw]).astype(BF16)
        for r in range(SSD_HEADS_PER_GROUP):
            h = g * SSD_HEADS_PER_GROUP + r
            seg = a_cs[:, h:h + 1] - a_cs_t[h:h + 1, :]
            decay = jnp.exp(jnp.where(causal, seg, -jnp.inf))
            m = (cb * decay).astype(BF16)
            y_ref[:, h * SSD_HEAD_DIM:(h + 1) * SSD_HEAD_DIM] = jnp.dot(
                m, xdt[:, r * SSD_HEAD_DIM:(r + 1) * SSD_HEAD_DIM], preferred_element_type=F32)
        st = state_ref[:, c0:c0 + gw]
        y_off = jnp.dot(cm16, st.astype(BF16), preferred_element_type=F32) * eax_ref[:, c0:c0 + gw]
        xdte = (xs * dtex_ref[:, c0:c0 + gw]).astype(BF16)
        state_ref[:, c0:c0 + gw] = decx[:, c0:c0 + gw] * st + jnp.dot(
            bm_t, xdte, preferred_element_type=F32)
        y = y_ref[:, c0:c0 + gw] + y_off + dskipx[:, c0:c0 + gw] * xs
        y = y * zs_ref[0, :, c0:c0 + gw].astype(F32)
        ms = jnp.mean(y * y, axis=-1, keepdims=True)
        o_ref[0, :, c0:c0 + gw] = (y * lax.rsqrt(ms + RMS_EPS)
                                   * normw_ref[:, c0:c0 + gw]).astype(o_ref.dtype)

    xpad_ref[0:8, :] = xpad_ref[q:q + 8, :]


def _ssd(xbc, zs, dt, conv_w, conv_b, a_log, d_skip, norm_w, expand):
    bsz, s, _ = xbc.shape
    nc = s // SSD_CHUNK
    full = lambda shape: pl.BlockSpec(shape, lambda b, c: (0, 0))
    return pl.pallas_call(
        _ssd_kernel,
        out_shape=jax.ShapeDtypeStruct((bsz, s, D_INNER), BF16),
        grid=(bsz, nc),
        in_specs=[pl.BlockSpec((1, SSD_CHUNK, CONV_DIM), lambda b, c: (b, c, 0)),
                  pl.BlockSpec((1, SSD_CHUNK, D_INNER), lambda b, c: (b, c, 0)),
                  pl.BlockSpec((1, SSD_CHUNK, HEAD_PAD), lambda b, c: (b, c, 0)),
                  full((SSD_CONV, CONV_DIM)), full((1, CONV_DIM)), full((1, HEAD_PAD)),
                  full((1, HEAD_PAD)), full((1, D_INNER)), full((HEAD_PAD, D_INNER))],
        out_specs=pl.BlockSpec((1, SSD_CHUNK, D_INNER), lambda b, c: (b, c, 0)),
        scratch_shapes=[pltpu.VMEM((SSD_CHUNK + 8, CONV_DIM), F32),
                        pltpu.VMEM((SSD_STATE, D_INNER), F32),
                        pltpu.VMEM((SSD_CHUNK, D_INNER), F32),
                        pltpu.VMEM((SSD_CHUNK, D_INNER), F32),
                        pltpu.VMEM((SSD_CHUNK, D_INNER), F32),
                        pltpu.VMEM((SSD_CHUNK, D_INNER), F32)],
        compiler_params=_cparams("parallel", "arbitrary"),
        name="ssd_mixer",
    )(xbc, zs, dt, conv_w, conv_b, a_log, d_skip, norm_w, expand)


ATT_STAGE_ROWS = 256


def _attn_kernel(q_ref, k_ref, v_ref, cos_ref, sin_ref, perm_ref, o_ref, lse_ref,
                 q32_ref, k32_ref, v32_ref, *, dilation):
    s = q_ref.shape[1]
    blk, dh, d = ATT_BLOCK, ATT_HEAD_DIM, dilation
    nb = s // d // blk
    perm = perm_ref[...]

    def stage(c, carry):
        rows = pl.ds(pl.multiple_of(c * ATT_STAGE_ROWS, ATT_STAGE_ROWS), ATT_STAGE_ROWS)
        cos, sin = cos_ref[0, rows, :], sin_ref[0, rows, :]
        for h in range(ATT_HEADS_PER_GROUP):
            sl = slice(h * dh, (h + 1) * dh)
            qh, kh = q_ref[0, rows, sl], k_ref[0, rows, sl]
            q_rot = qh.astype(F32) * cos + jnp.dot(qh, perm, preferred_element_type=F32) * sin
            q32_ref[h, rows, :] = q_rot * (dh ** -0.5)
            k32_ref[h, rows, :] = (kh.astype(F32) * cos
                                   + jnp.dot(kh, perm, preferred_element_type=F32) * sin)
            v32_ref[h, rows, :] = v_ref[0, rows, sl].astype(F32)
        return carry

    lax.fori_loop(0, s // ATT_STAGE_ROWS, stage, 0)

    qi = lax.broadcasted_iota(jnp.int32, (blk, blk), 0)
    kj = lax.broadcasted_iota(jnp.int32, (blk, blk), 1)
    mask_c = kj <= qi
    mask_p = kj >= qi
    qk = (((2,), (2,)), ((0,), (0,)))
    pv = (((2,), (1,)), ((0,), (0,)))

    units = [(r, n) for r in range(d) for n in range(nb)]

    def sub_rows(r, n):
        start = r + d * blk * n
        return pl.ds(start, blk, stride=d) if d > 1 else pl.ds(start, blk)

    def gather(ref, h, shift):
        return jnp.stack([ref[h, sub_rows(r, max(n - shift, 0)), :] for r, n in units]).astype(BF16)

    def head(h, carry):
        qb = gather(q32_ref, h, 0)
        kc, vc = gather(k32_ref, h, 0), gather(v32_ref, h, 0)
        s_c = jnp.where(mask_c[None], lax.dot_general(qb, kc, qk, preferred_element_type=F32),
                        -jnp.inf)
        m = jnp.max(s_c, axis=-1, keepdims=True)
        if nb > 1:
            kp, vp = gather(k32_ref, h, 1), gather(v32_ref, h, 1)
            has_prev = jnp.stack([jnp.full((blk, blk), n > 0) for _, n in units])
            s_p = lax.dot_general(qb, kp, qk, preferred_element_type=F32)
            s_p = jnp.where(has_prev, jnp.where(mask_p[None], s_p, -jnp.inf), -jnp.inf)
            m = jnp.maximum(m, jnp.max(s_p, axis=-1, keepdims=True))
        p_c = jnp.exp(s_c - m)
        den = jnp.sum(p_c, axis=-1, keepdims=True)
        acc = lax.dot_general(p_c.astype(BF16), vc, pv, preferred_element_type=F32)
        if nb > 1:
            p_p = jnp.exp(s_p - m)
            den = den + jnp.sum(p_p, axis=-1, keepdims=True)
            acc = acc + lax.dot_general(p_p.astype(BF16), vp, pv, preferred_element_type=F32)
        out = acc / den
        lse = jnp.broadcast_to(m + jnp.log(den), out.shape)
        for u, (r, n) in enumerate(units):
            o_ref[0, h, sub_rows(r, n), :] = out[u]
            lse_ref[0, h, sub_rows(r, n), :] = lse[u]
        return carry

    lax.fori_loop(0, ATT_HEADS_PER_GROUP, head, 0)


def _attention_group(qkv, cosf, sinf, perm, gi, dilation):
    bsz, s, _ = qkv.shape
    gw = ATT_HEADS_PER_GROUP * ATT_HEAD_DIM
    col = lambda which: pl.BlockSpec((1, s, gw), lambda b: (b, 0, which * 3 + gi))
    tab = pl.BlockSpec((1, s, LANES), lambda b: (b, 0, 0))
    out_spec = pl.BlockSpec((1, ATT_HEADS_PER_GROUP, s, ATT_HEAD_DIM), lambda b: (b, 0, 0, 0))
    shape = jax.ShapeDtypeStruct((bsz, ATT_HEADS_PER_GROUP, s, ATT_HEAD_DIM), F32)
    scratch = pltpu.VMEM((ATT_HEADS_PER_GROUP, s, ATT_HEAD_DIM), F32)
    return pl.pallas_call(
        functools.partial(_attn_kernel, dilation=dilation),
        out_shape=(shape, shape),
        grid=(bsz,),
        in_specs=[col(0), col(1), col(2), tab, tab, pl.BlockSpec((LANES, LANES), lambda b: (0, 0))],
        out_specs=(out_spec, out_spec),
        scratch_shapes=[scratch, scratch, scratch],
        compiler_params=_cparams("parallel"),
        name="dilated_attention_%d" % dilation,
    )(qkv, qkv, qkv, cosf, sinf, perm)


def _merge_kernel(y_ref, o1_ref, o2_ref, o3_ref, l1_ref, l2_ref, l3_ref, ws_ref, wa_ref,
                  gs_ref, ga_ref, out_ref, att_ref):
    @pl.when(pl.program_id(1) == 0)
    def _():
        for h in range(ATT_HEADS_PER_GROUP):
            l1, l2, l3 = l1_ref[0, h], l2_ref[0, h], l3_ref[0, h]
            m = jnp.maximum(jnp.maximum(l1, l2), l3)
            e1, e2, e3 = jnp.exp(l1 - m), jnp.exp(l2 - m), jnp.exp(l3 - m)
            num = e1 * o1_ref[0, h] + e2 * o2_ref[0, h] + e3 * o3_ref[0, h]
            att_ref[:, h * ATT_HEAD_DIM:(h + 1) * ATT_HEAD_DIM] = (num / (e1 + e2 + e3)).astype(BF16)

    y_ssd = jnp.dot(y_ref[...], ws_ref[...], preferred_element_type=F32)
    y_att = jnp.dot(att_ref[...], wa_ref[...], preferred_element_type=F32)
    out_ref[...] = (gs_ref[...].astype(F32) * y_ssd
                    + ga_ref[...].astype(F32) * y_att).astype(out_ref.dtype)


def _merge(y_ssd, outs, lses, w_ssd_br, w_att_br, gates, s, tm=512, tn=1024):
    t = y_ssd.shape[0]
    nj = D_MODEL // tn
    per_batch = s // tm
    head_major = pl.BlockSpec((1, ATT_HEADS_PER_GROUP, tm, ATT_HEAD_DIM),
                              lambda i, j: (i // per_batch, 0, i % per_batch, 0))
    return pl.pallas_call(
        _merge_kernel,
        out_shape=jax.ShapeDtypeStruct((t, D_MODEL), BF16),
        grid=(t // tm, nj),
        in_specs=[pl.BlockSpec((tm, D_INNER), lambda i, j: (i, 0))] + [head_major] * 6
                 + [pl.BlockSpec((D_INNER, tn), lambda i, j: (0, j)),
                    pl.BlockSpec((ATT_OUT, tn), lambda i, j: (0, j)),
                    pl.BlockSpec((tm, tn), lambda i, j: (i, j)),
                    pl.BlockSpec((tm, tn), lambda i, j: (i, j + nj))],
        out_specs=pl.BlockSpec((tm, tn), lambda i, j: (i, j)),
        scratch_shapes=[pltpu.VMEM((tm, ATT_OUT), BF16)],
        compiler_params=_cparams("parallel", "arbitrary"),
        name="branch_merge",
    )(y_ssd, *outs, *lses, w_ssd_br, w_att_br, gates, gates)


def _layer_norm(v, g, b):
    mu = jnp.mean(v, axis=-1, keepdims=True)
    c = v - mu
    var = jnp.mean(c * c, axis=-1, keepdims=True)
    return c * lax.rsqrt(var + LN_EPS) * g + b


def _pack_rows(v):
    half = D_MODEL // 2

    def bits(a):
        return lax.bitcast_convert_type(a.astype(BF16).astype(F32), jnp.uint32)

    return [(bits(v[:, half + s * LANES:half + (s + 1) * LANES]) & jnp.uint32(0xFFFF0000))
            | (bits(v[:, s * LANES:(s + 1) * LANES]) >> 16) for s in range(ROW_TILE)]


def _unpack_pair(c):
    return (lax.bitcast_convert_type(c << 16, F32),
            lax.bitcast_convert_type(c & jnp.uint32(0xFFFF0000), F32))


def _unpack_rows(chunks):
    pairs = [_unpack_pair(c) for c in chunks]
    return (jnp.concatenate([p[0] for p in pairs], axis=1),
            jnp.concatenate([p[1] for p in pairs], axis=1))


def _load_token_tiles(ref, row0, n_tokens):
    return [ref[pl.ds(row0 + s, n_tokens, stride=ROW_TILE), :] for s in range(ROW_TILE)]


def _store_token_tiles(ref, chunks):
    n_tokens = chunks[0].shape[0]
    for s, chunk in enumerate(chunks):
        ref[pl.ds(s, n_tokens, stride=ROW_TILE), :] = chunk


def _proj_ln_kernel(m_ref, w_ref, x_ref, g_ref, b_ref, o_ref, op_ref):
    v = DEEPNORM_ALPHA * x_ref[...] + jnp.dot(m_ref[...], w_ref[...], preferred_element_type=F32)
    out = _layer_norm(v, g_ref[...], b_ref[...])
    o_ref[...] = out
    _store_token_tiles(op_ref, _pack_rows(out))


def _proj_ln(merged, w_o, x, g, b, tm=256):
    t = x.shape[0]
    row = lambda: pl.BlockSpec((tm, D_MODEL), lambda i: (i, 0))
    const = lambda shape: pl.BlockSpec(shape, lambda i: (0, 0))
    return pl.pallas_call(
        _proj_ln_kernel,
        out_shape=(jax.ShapeDtypeStruct((t, D_MODEL), F32),
                   jax.ShapeDtypeStruct((t * ROW_TILE, LANES), jnp.uint32)),
        grid=(t // tm,),
        in_specs=[row(), const((D_MODEL, D_MODEL)), row(), const((1, D_MODEL)), const((1, D_MODEL))],
        out_specs=(row(), pl.BlockSpec((tm * ROW_TILE, LANES), lambda i: (i, 0))),
        compiler_params=_cparams("parallel"),
        name="out_proj_layernorm",
    )(merged, w_o, x, g, b)


def _router_kernel(x_ref, w_ref, bias_ref, idx_ref, wt_ref, rank_ref, cnt_ref, carry_ref):
    tm = x_ref.shape[0]
    ne, ng, eg = N_EXPERTS, N_EXPERT_GROUPS, EXPERTS_PER_GROUP

    @pl.when(pl.program_id(0) == 0)
    def _():
        carry_ref[...] = jnp.zeros_like(carry_ref)

    logits = lax.dot_general(w_ref[...], x_ref[...], (((1,), (1,)), ((), ())),
                             precision=lax.Precision.HIGHEST, preferred_element_type=F32)
    scores = _sigmoid(logits)
    sel = scores + bias_ref[...]
    sub = lax.broadcasted_iota(jnp.int32, (eg, tm), 0)
    neg = -jnp.inf

    gscore = []
    for g in range(ng):
        sg = sel[g * eg:(g + 1) * eg, :]
        m1 = jnp.max(sg, axis=0, keepdims=True)
        first = jnp.min(jnp.where(sg == m1, sub, eg), axis=0, keepdims=True)
        m2 = jnp.max(jnp.where(sub == first, neg, sg), axis=0, keepdims=True)
        gscore.append(m1 + m2)
    keep = []
    for g in range(ng):
        beaten = jnp.zeros((1, tm), jnp.int32)
        for o in range(ng):
            if o == g:
                continue
            wins = (gscore[o] >= gscore[g]) if o < g else (gscore[o] > gscore[g])
            beaten = beaten + wins.astype(jnp.int32)
        keep.append(beaten < TOPK_GROUPS)
    masked = jnp.concatenate(
        [jnp.where(keep[g], sel[g * eg:(g + 1) * eg, :], neg) for g in range(ng)], axis=0)

    eidx = lax.broadcasted_iota(jnp.int32, (ne, tm), 0)
    picks, weights, chosen_masks = [], [], []
    for _ in range(TOP_K):
        mx = jnp.max(masked, axis=0, keepdims=True)
        pick = jnp.min(jnp.where(masked == mx, eidx, ne), axis=0, keepdims=True)
        chosen = eidx == pick
        weights.append(jnp.sum(jnp.where(chosen, scores, 0.0), axis=0, keepdims=True))
        picks.append(pick)
        chosen_masks.append(chosen)
        masked = jnp.where(chosen, neg, masked)
    w = jnp.concatenate(weights, axis=0)
    idx_ref[...] = jnp.concatenate(picks, axis=0)
    wt_ref[...] = w / jnp.sum(w, axis=0, keepdims=True) * ROUTED_SCALE

    cnt = jnp.zeros((ne, tm), F32)
    for chosen in chosen_masks:
        cnt = cnt + jnp.where(chosen, 1.0, 0.0)
    earlier = (lax.broadcasted_iota(jnp.int32, (tm, tm), 0)
               < lax.broadcasted_iota(jnp.int32, (tm, tm), 1))
    before = jnp.dot(cnt.astype(BF16), jnp.where(earlier, 1.0, 0.0).astype(BF16),
                     preferred_element_type=F32) + carry_ref[...]
    rank_ref[...] = jnp.concatenate(
        [jnp.sum(jnp.where(chosen, before, 0.0), axis=0, keepdims=True) for chosen in chosen_masks],
        axis=0).astype(jnp.int32)
    carry_ref[...] = carry_ref[...] + jnp.sum(cnt, axis=1, keepdims=True)
    cnt_ref[...] = carry_ref[...]


def _router(x, w_router_t, bias, tm=512):
    t = x.shape[0]
    tok = lambda: pl.BlockSpec((TOP_K, tm), lambda i: (0, i))
    return pl.pallas_call(
        _router_kernel,
        out_shape=(jax.ShapeDtypeStruct((TOP_K, t), jnp.int32),
                   jax.ShapeDtypeStruct((TOP_K, t), F32),
                   jax.ShapeDtypeStruct((TOP_K, t), jnp.int32),
                   jax.ShapeDtypeStruct((N_EXPERTS, 1), F32)),
        grid=(t // tm,),
        in_specs=[pl.BlockSpec((tm, D_MODEL), lambda i: (i, 0)),
                  pl.BlockSpec((N_EXPERTS, D_MODEL), lambda i: (0, 0)),
                  pl.BlockSpec((N_EXPERTS, 1), lambda i: (0, 0))],
        out_specs=(tok(), tok(), tok(), pl.BlockSpec((N_EXPERTS, 1), lambda i: (0, 0))),
        scratch_shapes=[pltpu.VMEM((N_EXPERTS, 1), F32)],
        compiler_params=_cparams("arbitrary"),
        name="router_topk",
    )(x, w_router_t, bias)


def _gather_tiles(src_hbm, idx_ref, dst_ref, sem, r0, r1, priority=0):
    for r in range(r0, r1):
        first = pl.multiple_of(idx_ref[0, 0, r], ROW_TILE)
        pltpu.make_async_copy(src_hbm.at[pl.ds(first, ROW_TILE), :],
                              dst_ref.at[pl.ds(r * ROW_TILE, ROW_TILE), :], sem).start(priority)


def _wait_tiles(src_hbm, dst_ref, sem, n_tokens):
    pltpu.make_async_copy(src_hbm.at[pl.ds(0, n_tokens * ROW_TILE), :], dst_ref, sem).wait()


def _experts_kernel(be_ref, na_ref, tok0_ref, tokn_ref, x_hbm, wrow_ref, wg_ref, wu_ref, wd_ref,
                    y_ref, buf_ref, sem_ref, wg16_ref, wu16_ref, wd16_ref, *, priority):
    i = pl.program_id(0)
    n_active = na_ref[0]
    slot = lax.rem(i, 2)
    half = D_MODEL // 2

    @pl.when((i == 0) & (n_active > 0))
    def _():
        _gather_tiles(x_hbm, tok0_ref, buf_ref.at[0], sem_ref.at[0], 0, MOE_ROWS, priority)

    new_expert = (i == 0) | (be_ref[i] != be_ref[jnp.maximum(i - 1, 0)])

    @pl.when(new_expert & (i < n_active))
    def _():
        wg16_ref[...] = wg_ref[0].astype(BF16)
        wu16_ref[...] = wu_ref[0].astype(BF16)
        wd16_ref[...] = wd_ref[0].astype(BF16)

    @pl.when(i < n_active)
    def _():
        cur, nxt = buf_ref.at[slot], buf_ref.at[1 - slot]
        _wait_tiles(x_hbm, cur, sem_ref.at[slot], MOE_ROWS)
        group = MOE_ROWS // ROW_TILE
        gate = up = None
        for pair in range(ROW_TILE // 2):
            lo, hi = [], []
            for s in (2 * pair, 2 * pair + 1):
                _gather_tiles(x_hbm, tokn_ref, nxt, sem_ref.at[1 - slot], s * group, (s + 1) * group,
                              priority)
                c_lo, c_hi = _unpack_pair(cur[pl.ds(s, MOE_ROWS, stride=ROW_TILE), :])
                lo.append(c_lo.astype(BF16))
                hi.append(c_hi.astype(BF16))
            x_lo, x_hi = jnp.concatenate(lo, axis=1), jnp.concatenate(hi, axis=1)
            k_lo = slice(2 * pair * LANES, (2 * pair + 2) * LANES)
            k_hi = slice(half + 2 * pair * LANES, half + (2 * pair + 2) * LANES)
            g = (jnp.dot(x_lo, wg16_ref[k_lo, :], preferred_element_type=F32)
                 + jnp.dot(x_hi, wg16_ref[k_hi, :], preferred_element_type=F32))
            u = (jnp.dot(x_lo, wu16_ref[k_lo, :], preferred_element_type=F32)
                 + jnp.dot(x_hi, wu16_ref[k_hi, :], preferred_element_type=F32))
            gate = g if gate is None else gate + g
            up = u if up is None else up + u
        h = gate * _sigmoid(gate) * up * wrow_ref[...]
        y = jnp.dot(h.astype(BF16), wd16_ref[...], preferred_element_type=F32)
        _store_token_tiles(y_ref, _pack_rows(y))

    @pl.when(i == n_active - 1)
    def _():
        _wait_tiles(x_hbm, buf_ref.at[1 - slot], sem_ref.at[1 - slot], MOE_ROWS)

    @pl.when(i >= n_active)
    def _():
        y_ref[...] = jnp.zeros_like(y_ref)


def _experts(xp, block_e, n_active, row_tok, row_w, w_gate, w_up, w_down, layer, priority):
    n_blocks = row_tok.shape[0]
    n_rows = n_blocks * MOE_ROWS
    smem_blk = lambda index_map: pl.BlockSpec((1, 1, MOE_ROWS), index_map, memory_space=pltpu.SMEM)
    wspec = lambda a, b: pl.BlockSpec((1, 1, a, b), lambda i, be, na: (layer, be[i], 0, 0))

    def kernel(be_ref, na_ref, tok0_ref, tokn_ref, x_hbm, wrow_ref, wg_ref, wu_ref, wd_ref, *rest):
        _experts_kernel(be_ref, na_ref, tok0_ref, tokn_ref, x_hbm, wrow_ref, wg_ref.at[0],
                        wu_ref.at[0], wd_ref.at[0], *rest, priority=priority)

    grid_spec = pltpu.PrefetchScalarGridSpec(
        num_scalar_prefetch=2,
        grid=(n_blocks,),
        in_specs=[smem_blk(lambda i, be, na: (0, 0, 0)),
                  smem_blk(lambda i, be, na: (jnp.minimum(i + 1, n_blocks - 1), 0, 0)),
                  pl.BlockSpec(memory_space=pl.ANY),
                  pl.BlockSpec((MOE_ROWS, 1), lambda i, be, na: (i, 0)),
                  wspec(D_MODEL, D_EXPERT), wspec(D_MODEL, D_EXPERT), wspec(D_EXPERT, D_MODEL)],
        out_specs=pl.BlockSpec((MOE_ROWS * ROW_TILE, LANES), lambda i, be, na: (i, 0)),
        scratch_shapes=[pltpu.VMEM((2, MOE_ROWS * ROW_TILE, LANES), jnp.uint32),
                        pltpu.SemaphoreType.DMA((2,)),
                        pltpu.VMEM((D_MODEL, D_EXPERT), BF16),
                        pltpu.VMEM((D_MODEL, D_EXPERT), BF16),
                        pltpu.VMEM((D_EXPERT, D_MODEL), BF16)])
    return pl.pallas_call(
        kernel,
        out_shape=jax.ShapeDtypeStruct((n_rows * ROW_TILE, LANES), jnp.uint32),
        grid_spec=grid_spec,
        compiler_params=_cparams("arbitrary"),
        name="routed_experts",
    )(block_e, n_active, row_tok, row_tok, xp, row_w, w_gate, w_up, w_down)


def _combine_kernel(d0_ref, dn_ref, y_hbm, x_ref, wg_ref, wu_ref, wd_ref, g_ref, b_ref,
                    o_ref, o16_ref, buf_ref, sem_ref, *, priority):
    i = pl.program_id(0)
    n = pl.num_programs(0)
    slot = lax.rem(i, 2)
    tm = COMBINE_TOKENS
    rows = tm * TOP_K

    @pl.when(i == 0)
    def _():
        _gather_tiles(y_hbm, d0_ref, buf_ref.at[0], sem_ref.at[0], 0, rows, priority)

    _wait_tiles(y_hbm, buf_ref.at[slot], sem_ref.at[slot], rows)
    _gather_tiles(y_hbm, dn_ref, buf_ref.at[1 - slot], sem_ref.at[1 - slot], 0, rows, priority)

    x = x_ref[...]
    x16 = x.astype(BF16)
    gate = jnp.dot(x16, wg_ref[...], preferred_element_type=F32)
    up = jnp.dot(x16, wu_ref[...], preferred_element_type=F32)
    h = (gate * _sigmoid(gate) * up).astype(BF16)
    v = DEEPNORM_ALPHA * x + jnp.dot(h, wd_ref[...], preferred_element_type=F32)

    lo = [None] * ROW_TILE
    hi = [None] * ROW_TILE
    for k in range(TOP_K):
        chunks = _load_token_tiles(buf_ref.at[slot], k * tm * ROW_TILE, tm)
        for s, c in enumerate(chunks):
            c_lo, c_hi = _unpack_pair(c)
            lo[s] = c_lo if k == 0 else lo[s] + c_lo
            hi[s] = c_hi if k == 0 else hi[s] + c_hi
    v = v + jnp.concatenate(lo + hi, axis=1)
    out = _layer_norm(v, g_ref[...], b_ref[...])
    o_ref[...] = out
    o16_ref[...] = out.astype(BF16)

    @pl.when(i == n - 1)
    def _():
        _wait_tiles(y_hbm, buf_ref.at[1 - slot], sem_ref.at[1 - slot], rows)


def _combine(y_rows, dest_blocks, x, w_gate_s, w_up_s, w_down_s, g, b, priority):
    t = x.shape[0]
    tm = COMBINE_TOKENS
    rows = tm * TOP_K
    row = lambda: pl.BlockSpec((tm, D_MODEL), lambda i: (i, 0))
    const = lambda shape: pl.BlockSpec(shape, lambda i: (0, 0))
    nblk = t // tm
    smem_blk = lambda index_map: pl.BlockSpec((1, 1, rows), index_map, memory_space=pltpu.SMEM)
    return pl.pallas_call(
        functools.partial(_combine_kernel, priority=priority),
        out_shape=(jax.ShapeDtypeStruct((t, D_MODEL), F32), jax.ShapeDtypeStruct((t, D_MODEL), BF16)),
        grid=(nblk,),
        in_specs=[smem_blk(lambda i: (0, 0, 0)),
                  smem_blk(lambda i: (jnp.minimum(i + 1, nblk - 1), 0, 0)),
                  pl.BlockSpec(memory_space=pl.ANY),
                  row(),
                  const((D_MODEL, D_SHARED)), const((D_MODEL, D_SHARED)), const((D_SHARED, D_MODEL)),
                  const((1, D_MODEL)), const((1, D_MODEL))],
        out_specs=(row(), row()),
        scratch_shapes=[pltpu.VMEM((2, rows * ROW_TILE, LANES), jnp.uint32),
                        pltpu.SemaphoreType.DMA((2,))],
        compiler_params=_cparams("arbitrary"),
        name="moe_combine_layernorm",
    )(dest_blocks, dest_blocks, y_rows, x, w_gate_s, w_up_s, w_down_s, g, b)


def _dispatch_plan(idx_t, wt_t, rank_t, counts):
    t = idx_t.shape[1]
    tk = t * TOP_K
    n_blocks = tk // MOE_ROWS + N_EXPERTS
    experts = jnp.arange(N_EXPERTS, dtype=jnp.int32)
    counts = counts.reshape(N_EXPERTS).astype(jnp.int32)
    padded = (counts + MOE_ROWS - 1) // MOE_ROWS * MOE_ROWS
    pad_end = jnp.cumsum(padded)
    pad_start = pad_end - padded
    start = jnp.cumsum(counts) - counts
    first_row = jnp.sum(jnp.where(idx_t[None] == experts[:, None, None],
                                  pad_start[:, None, None], 0), axis=0)
    dest_t = (first_row + rank_t).astype(jnp.int32)
    tok = jnp.broadcast_to(jnp.arange(t, dtype=jnp.int32)[None, :], (TOP_K, t))
    _, sorted_tok, sorted_w = lax.sort((dest_t.reshape(-1), tok.reshape(-1), wt_t.reshape(-1)),
                                       num_keys=1)
    block_first = jnp.arange(n_blocks, dtype=jnp.int32) * MOE_ROWS
    block_e = jnp.minimum(jnp.sum(pad_end[None, :] <= block_first[:, None], axis=1),
                          N_EXPERTS - 1).astype(jnp.int32)
    n_active = (pad_end[-1] // MOE_ROWS).astype(jnp.int32).reshape(1)
    within = (block_first - pad_start[block_e])[:, None] + jnp.arange(MOE_ROWS, dtype=jnp.int32)[None, :]
    valid = (within < counts[block_e][:, None]) & (block_first < pad_end[-1])[:, None]
    pos = jnp.clip(start[block_e][:, None] + within, 0, tk - 1)
    row_tok = (jnp.where(valid, sorted_tok[pos], 0) * ROW_TILE).astype(jnp.int32)
    row_w = jnp.where(valid, sorted_w[pos], 0.0).astype(F32)
    return (block_e, n_active, row_tok.reshape(n_blocks, 1, MOE_ROWS),
            row_w.reshape(n_blocks * MOE_ROWS, 1), dest_t)


def _layer(x, x16, cosf, sinf, perm, expand, bsz, s, l, w_in, w_tail, w_gate_e, w_up_e, w_down_e, p):
    t = bsz * s
    zs = _matmul(x16, w_in, l, 0, D_INNER, p["zero_z"], "silu", BF16, 1024, 1024)
    xbc = _matmul(x16, w_in, l, D_INNER, CONV_DIM, p["zero_xbc"], "none", BF16, 1024, 1024)
    dt = _matmul(x16, w_in, l, DT_COL, HEAD_PAD, p["dt_bias"], "softplus", F32, 1024, HEAD_PAD)
    qkv = _matmul(x16, w_tail, l, 0, 3 * ATT_WIDTH, p["zero_qkv"], "none", BF16, 1024, 768)
    gates = _matmul(x16, w_tail, l, 3 * ATT_WIDTH, 2 * D_MODEL, p["gate_b"], "sigmoid", BF16, 1024, 512)

    y_ssd = _ssd(xbc.reshape(bsz, s, CONV_DIM), zs.reshape(bsz, s, D_INNER),
                 dt.reshape(bsz, s, HEAD_PAD), p["conv_w"], p["conv_b"], p["a_log"], p["d_skip"],
                 p["norm_w"], expand).reshape(t, D_INNER)

    qkv3 = qkv.reshape(bsz, s, 3 * ATT_WIDTH)
    outs, lses = [], []
    for gi, (_, dilation) in enumerate(ATT_GROUPS):
        o, lse = _attention_group(qkv3, cosf, sinf, perm, gi, dilation)
        outs.append(o)
        lses.append(lse)

    merged = _merge(y_ssd, outs, lses, p["w_ssd_br"], p["w_att_br"], gates, s)
    x1, x1p = _proj_ln(merged, p["w_o"], x, p["ln1_g"], p["ln1_b"])

    idx_t, wt_t, rank_t, counts = _router(x1, p["w_router_t"], p["router_bias"])
    block_e, n_active, row_tok, row_w, dest_t = _dispatch_plan(idx_t, wt_t, rank_t, counts)
    row_dma_priority = 1 if l == 0 else 0
    y_rows = _experts(x1p, block_e, n_active, row_tok, row_w, w_gate_e, w_up_e, w_down_e, l,
                      row_dma_priority)
    nt = t // COMBINE_TOKENS
    dest_blocks = (dest_t * ROW_TILE).reshape(TOP_K, nt, COMBINE_TOKENS).transpose(1, 0, 2)
    dest_blocks = dest_blocks.reshape(nt, 1, TOP_K * COMBINE_TOKENS)
    return _combine(y_rows, dest_blocks, x1, p["w_gate_s"], p["w_up_s"], p["w_down_s"],
                    p["ln2_g"], p["ln2_b"], row_dma_priority)


def _pad_heads(v):
    return jnp.pad(v.astype(F32), (0, HEAD_PAD - SSD_HEADS)).reshape(1, HEAD_PAD)


def _layer_params(l, conv_w, conv_b, dt_bias, a_log, d_skip, ssd_norm_w, w_ssd_br, w_att_br,
                  gate_b, w_o, ln1_g, ln1_b, w_router, router_bias, w_gate_s, w_up_s, w_down_s,
                  ln2_g, ln2_b):
    row = lambda v: v.astype(F32).reshape(1, -1)
    return dict(
        zero_z=jnp.zeros((1, D_INNER), F32), zero_xbc=jnp.zeros((1, CONV_DIM), F32),
        zero_qkv=jnp.zeros((1, 3 * ATT_WIDTH), F32),
        dt_bias=_pad_heads(dt_bias[l]), gate_b=row(gate_b[l]),
        conv_w=conv_w[l].astype(F32), conv_b=row(conv_b[l]),
        a_log=_pad_heads(a_log[l]), d_skip=_pad_heads(d_skip[l]), norm_w=row(ssd_norm_w[l]),
        w_ssd_br=w_ssd_br[l].astype(BF16), w_att_br=w_att_br[l].astype(BF16),
        w_o=w_o[l].astype(BF16), ln1_g=row(ln1_g[l]), ln1_b=row(ln1_b[l]),
        w_router_t=w_router[l].astype(F32).T, router_bias=router_bias[l].astype(F32).reshape(-1, 1),
        w_gate_s=w_gate_s[l].astype(BF16), w_up_s=w_up_s[l].astype(BF16),
        w_down_s=w_down_s[l].astype(BF16), ln2_g=row(ln2_g[l]), ln2_b=row(ln2_b[l]))


def _rope_tables(positions):
    half = ROPE_DIM // 2
    inv_freq = ROPE_THETA ** (-jnp.arange(0, ROPE_DIM, 2, dtype=F32) / ROPE_DIM)
    ang = positions.astype(F32)[..., None] * inv_freq
    cos, sin = jnp.cos(ang), jnp.sin(ang)
    pad = positions.shape + (ATT_HEAD_DIM - ROPE_DIM,)
    cosf = jnp.concatenate([cos, cos, jnp.ones(pad, F32)], axis=-1)
    sinf = jnp.concatenate([-sin, sin, jnp.zeros(pad, F32)], axis=-1)
    k = jnp.arange(ATT_HEAD_DIM)[:, None]
    j = jnp.arange(ATT_HEAD_DIM)[None, :]
    perm = ((j < half) & (k == j + half)) | ((j >= half) & (j < ROPE_DIM) & (k == j - half))
    return cosf, sinf, perm.astype(BF16)


def kernel(x, positions, w_in, conv_w, conv_b, dt_bias, a_log, d_skip, ssd_norm_w, w_ssd_br,
           w_att_br, gate_b, w_o, ln1_g, ln1_b, w_router, router_bias, w_gate_e, w_up_e, w_down_e,
           w_gate_s, w_up_s, w_down_s, ln2_g, ln2_b):
    bsz, s, d = x.shape
    cosf, sinf, perm = _rope_tables(positions)
    head_of_col = jnp.arange(D_INNER) // SSD_HEAD_DIM
    expand = (jnp.arange(HEAD_PAD)[:, None] == head_of_col[None, :]).astype(BF16)
    w_in = w_in.astype(F32)
    w_tail = w_in[:, :, TAIL_COL:]
    w_gate_e, w_up_e, w_down_e = (w.astype(F32) for w in (w_gate_e, w_up_e, w_down_e))
    xf = x.reshape(bsz * s, d).astype(F32)
    x16 = xf.astype(BF16)
    for l in range(DEPTH):
        p = _layer_params(l, conv_w, conv_b, dt_bias, a_log, d_skip, ssd_norm_w, w_ssd_br, w_att_br,
                          gate_b, w_o, ln1_g, ln1_b, w_router, router_bias, w_gate_s, w_up_s,
                          w_down_s, ln2_g, ln2_b)
        xf, x16 = _layer(xf, x16, cosf, sinf, perm, expand, bsz, s, l, w_in, w_tail,
                         w_gate_e, w_up_e, w_down_e, p)
    return xf.reshape(bsz, s, d).astype(x.dtype)
```

```python
import functools

import jax
import jax.numpy as jnp
from jax import lax
from jax.experimental import pallas as pl
from jax.experimental.pallas import tpu as pltpu

D_MODEL = 2048
DEPTH = 2

D_INNER = 4096
SSD_HEAD_DIM = 64
SSD_HEADS = 64
SSD_GROUPS = 8
SSD_HEADS_PER_GROUP = 8
SSD_STATE = 128
SSD_CONV = 4
SSD_CHUNK = 128
CONV_DIM = D_INNER + 2 * SSD_GROUPS * SSD_STATE
RMS_EPS = 1e-5

ATT_HEAD_DIM = 128
ATT_GROUPS = ((128, 1), (512, 4), (2048, 16))
ATT_HEADS_PER_GROUP = 4
ATT_WIDTH = 1536
ATT_OUT = 512
ATT_BLOCK = 128
ROPE_THETA = 500000.0
ROPE_DIM = 32

IN_SIZES = (D_INNER, CONV_DIM, SSD_HEADS, ATT_WIDTH, ATT_WIDTH, ATT_WIDTH, D_MODEL, D_MODEL)
DT_COL = D_INNER + CONV_DIM
TAIL_COL = DT_COL + SSD_HEADS

N_EXPERTS = 64
N_EXPERT_GROUPS = 8
EXPERTS_PER_GROUP = 8
TOPK_GROUPS = 4
TOP_K = 8
D_EXPERT = 512
D_SHARED = 512
ROUTED_SCALE = 2.5

DEEPNORM_ALPHA = (2 * DEPTH) ** 0.25
LN_EPS = 1e-5

LANES = 128
HEAD_PAD = 128
ROW_TILE = 8
MOE_ROWS = 256
COMBINE_TOKENS = 128
VMEM_LIMIT = 56 * 1024 * 1024

F32 = jnp.float32
BF16 = jnp.bfloat16


def _cparams(*sem):
    return pltpu.CompilerParams(dimension_semantics=sem, vmem_limit_bytes=VMEM_LIMIT)


def _sigmoid(v):
    return 1.0 / (1.0 + jnp.exp(-v))


def _mm_kernel(x_ref, w_ref, b_ref, o_ref, *scratch, act):
    if scratch:
        w16_ref, = scratch

        @pl.when(pl.program_id(1) == 0)
        def _():
            w16_ref[...] = w_ref[0].astype(BF16)
        w = w16_ref[...]
    else:
        w = w_ref[0]

    acc = jnp.dot(x_ref[...], w, preferred_element_type=F32) + b_ref[...]
    if act == "silu":
        acc = acc * _sigmoid(acc)
    elif act == "sigmoid":
        acc = _sigmoid(acc)
    elif act == "softplus":
        acc = jnp.maximum(acc, 0.0) + jnp.log(1.0 + jnp.exp(-jnp.abs(acc)))
    o_ref[...] = acc.astype(o_ref.dtype)


def _matmul(x, w, layer, col0, n, bias, act, out_dtype, tm, tn):
    m, k = x.shape
    assert col0 % tn == 0 and n % tn == 0 and m % tm == 0
    j0 = col0 // tn
    scratch = [] if w.dtype == BF16 else [pltpu.VMEM((k, tn), BF16)]
    return pl.pallas_call(
        functools.partial(_mm_kernel, act=act),
        out_shape=jax.ShapeDtypeStruct((m, n), out_dtype),
        grid=(n // tn, m // tm),
        in_specs=[pl.BlockSpec((tm, k), lambda j, i: (i, 0)),
                  pl.BlockSpec((1, k, tn), lambda j, i: (layer, 0, j0 + j)),
                  pl.BlockSpec((1, tn), lambda j, i: (0, j))],
        out_specs=pl.BlockSpec((tm, tn), lambda j, i: (i, j)),
        scratch_shapes=scratch,
        compiler_params=_cparams("parallel", "arbitrary"),
        name="proj_" + act,
    )(x, w, bias)


def _split_dot(v, e):
    hi = v.astype(BF16)
    lo = (v - hi.astype(F32)).astype(BF16)
    return (jnp.dot(hi, e, preferred_element_type=F32)
            + jnp.dot(lo, e, preferred_element_type=F32))


def _split_dot_left(e, v):
    hi = v.astype(BF16)
    lo = (v - hi.astype(F32)).astype(BF16)
    return (jnp.dot(e, hi, preferred_element_type=F32)
            + jnp.dot(e, lo, preferred_element_type=F32))


def _ssd_kernel(xbc_ref, zs_ref, dt_ref, convw_ref, convb_ref, alog_ref, dskip_ref, normw_ref,
                expand_ref, o_ref, xpad_ref, state_ref, y_ref, dtx_ref, dtex_ref, eax_ref):
    q = SSD_CHUNK
    gw = SSD_HEADS_PER_GROUP * SSD_HEAD_DIM

    @pl.when(pl.program_id(1) == 0)
    def _():
        xpad_ref[0:8, :] = jnp.zeros((8, CONV_DIM), F32)
        state_ref[...] = jnp.zeros_like(state_ref)

    xpad_ref[8:8 + q, :] = xbc_ref[0].astype(F32)

    def conv(c0, width):
        acc = convb_ref[:, c0:c0 + width]
        for j in range(SSD_CONV):
            lo = 8 - (SSD_CONV - 1) + j
            acc = acc + convw_ref[j:j + 1, c0:c0 + width] * xpad_ref[lo:lo + q, c0:c0 + width]
        return acc * _sigmoid(acc)

    row = lax.broadcasted_iota(jnp.int32, (q, q), 0)
    col = lax.broadcasted_iota(jnp.int32, (q, q), 1)
    causal = col <= row
    tril = jnp.where(causal, 1.0, 0.0).astype(BF16)

    dt = dt_ref[0]
    a = -jnp.exp(alog_ref[...])
    da = dt * a
    a_cs = _split_dot_left(tril, da)
    a_cs_t = a_cs.T
    a_last = a_cs[q - 1:q, :]
    expand = expand_ref[...]
    dtx_ref[...] = _split_dot(dt, expand)
    dtex_ref[...] = _split_dot(dt * jnp.exp(a_last - a_cs), expand)
    eax_ref[...] = _split_dot(jnp.exp(a_cs), expand)
    decx = _split_dot(jnp.broadcast_to(jnp.exp(a_last), (8, HEAD_PAD)), expand)[0:1, :]
    dskipx = _split_dot(jnp.broadcast_to(dskip_ref[...], (8, HEAD_PAD)), expand)[0:1, :]

    for g in range(SSD_GROUPS):
        c0 = g * gw
        xs = conv(c0, gw)
        bm = conv(D_INNER + g * SSD_STATE, SSD_STATE)
        cm = conv(D_INNER + SSD_GROUPS * SSD_STATE + g * SSD_STATE, SSD_STATE)
        bm_t = bm.T.astype(BF16)
        cm16 = cm.astype(BF16)
        cb = jnp.dot(cm16, bm_t, preferred_element_type=F32)
        xdt = (xs * dtx_ref[:, c0:c0 + gw]).astype(BF16)
        for r in range(SSD_HEADS_PER_GROUP):
            h = g * SSD_HEADS_PER_GROUP + r
            seg = a_cs[:, h:h + 1] - a_cs_t[h:h + 1, :]
            decay = jnp.exp(jnp.where(causal, seg, -jnp.inf))
            m = (cb * decay).astype(BF16)
            y_ref[:, h * SSD_HEAD_DIM:(h + 1) * SSD_HEAD_DIM] = jnp.dot(
                m, xdt[:, r * SSD_HEAD_DIM:(r + 1) * SSD_HEAD_DIM], preferred_element_type=F32)
        st = state_ref[:, c0:c0 + gw]
        y_off = jnp.dot(cm16, st.astype(BF16), preferred_element_type=F32) * eax_ref[:, c0:c0 + gw]
        xdte = (xs * dtex_ref[:, c0:c0 + gw]).astype(BF16)
        state_ref[:, c0:c0 + gw] = decx[:, c0:c0 + gw] * st + jnp.dot(
            bm_t, xdte, preferred_element_type=F32)
        y = y_ref[:, c0:c0 + gw] + y_off + dskipx[:, c0:c0 + gw] * xs
        y = y * zs_ref[0, :, c0:c0 + gw].astype(F32)
        ms = jnp.mean(y * y, axis=-1, keepdims=True)
        o_ref[0, :, c0:c0 + gw] = (y * lax.rsqrt(ms + RMS_EPS)
                                   * normw_ref[:, c0:c0 + gw]).astype(o_ref.dtype)

    xpad_ref[0:8, :] = xpad_ref[q:q + 8, :]


def _ssd(xbc, zs, dt, conv_w, conv_b, a_log, d_skip, norm_w, expand):
    bsz, s, _ = xbc.shape
    nc = s // SSD_CHUNK
    full = lambda shape: pl.BlockSpec(shape, lambda b, c: (0, 0))
    return pl.pallas_call(
        _ssd_kernel,
        out_shape=jax.ShapeDtypeStruct((bsz, s, D_INNER), BF16),
        grid=(bsz, nc),
        in_specs=[pl.BlockSpec((1, SSD_CHUNK, CONV_DIM), lambda b, c: (b, c, 0)),
                  pl.BlockSpec((1, SSD_CHUNK, D_INNER), lambda b, c: (b, c, 0)),
                  pl.BlockSpec((1, SSD_CHUNK, HEAD_PAD), lambda b, c: (b, c, 0)),
                  full((SSD_CONV, CONV_DIM)), full((1, CONV_DIM)), full((1, HEAD_PAD)),
                  full((1, HEAD_PAD)), full((1, D_INNER)), full((HEAD_PAD, D_INNER))],
        out_specs=pl.BlockSpec((1, SSD_CHUNK, D_INNER), lambda b, c: (b, c, 0)),
        scratch_shapes=[pltpu.VMEM((SSD_CHUNK + 8, CONV_DIM), F32),
                        pltpu.VMEM((SSD_STATE, D_INNER), F32),
                        pltpu.VMEM((SSD_CHUNK, D_INNER), F32),
                        pltpu.VMEM((SSD_CHUNK, D_INNER), F32),
                        pltpu.VMEM((SSD_CHUNK, D_INNER), F32),
                        pltpu.VMEM((SSD_CHUNK, D_INNER), F32)],
        compiler_params=_cparams("parallel", "arbitrary"),
        name="ssd_mixer",
    )(xbc, zs, dt, conv_w, conv_b, a_log, d_skip, norm_w, expand)


ATT_STAGE_ROWS = 256


def _attn_kernel(q_ref, k_ref, v_ref, cos_ref, sin_ref, perm_ref, o_ref, lse_ref,
                 q32_ref, k32_ref, v32_ref, *, dilation):
    s = q_ref.shape[1]
    blk, dh, d = ATT_BLOCK, ATT_HEAD_DIM, dilation
    nb = s // d // blk
    perm = perm_ref[...]

    def stage(c, carry):
        rows = pl.ds(pl.multiple_of(c * ATT_STAGE_ROWS, ATT_STAGE_ROWS), ATT_STAGE_ROWS)
        cos, sin = cos_ref[0, rows, :], sin_ref[0, rows, :]
        for h in range(ATT_HEADS_PER_GROUP):
            sl = slice(h * dh, (h + 1) * dh)
            qh, kh = q_ref[0, rows, sl], k_ref[0, rows, sl]
            q_rot = qh.astype(F32) * cos + jnp.dot(qh, perm, preferred_element_type=F32) * sin
            q32_ref[h, rows, :] = q_rot * (dh ** -0.5)
            k32_ref[h, rows, :] = (kh.astype(F32) * cos
                                   + jnp.dot(kh, perm, preferred_element_type=F32) * sin)
            v32_ref[h, rows, :] = v_ref[0, rows, sl].astype(F32)
        return carry

    lax.fori_loop(0, s // ATT_STAGE_ROWS, stage, 0)

    qi = lax.broadcasted_iota(jnp.int32, (blk, blk), 0)
    kj = lax.broadcasted_iota(jnp.int32, (blk, blk), 1)
    mask_c = kj <= qi
    mask_p = kj >= qi
    qk = (((2,), (2,)), ((0,), (0,)))
    pv = (((2,), (1,)), ((0,), (0,)))

    units = [(r, n) for r in range(d) for n in range(nb)]

    def sub_rows(r, n):
        start = r + d * blk * n
        return pl.ds(start, blk, stride=d) if d > 1 else pl.ds(start, blk)

    def gather(ref, h, shift):
        return jnp.stack([ref[h, sub_rows(r, max(n - shift, 0)), :] for r, n in units]).astype(BF16)

    def head(h, carry):
        qb = gather(q32_ref, h, 0)
        kc, vc = gather(k32_ref, h, 0), gather(v32_ref, h, 0)
        s_c = jnp.where(mask_c[None], lax.dot_general(qb, kc, qk, preferred_element_type=F32),
                        -jnp.inf)
        m = jnp.max(s_c, axis=-1, keepdims=True)
        if nb > 1:
            kp, vp = gather(k32_ref, h, 1), gather(v32_ref, h, 1)
            has_prev = jnp.stack([jnp.full((blk, blk), n > 0) for _, n in units])
            s_p = lax.dot_general(qb, kp, qk, preferred_element_type=F32)
            s_p = jnp.where(has_prev, jnp.where(mask_p[None], s_p, -jnp.inf), -jnp.inf)
            m = jnp.maximum(m, jnp.max(s_p, axis=-1, keepdims=True))
        p_c = jnp.exp(s_c - m)
        den = jnp.sum(p_c, axis=-1, keepdims=True)
        acc = lax.dot_general(p_c.astype(BF16), vc, pv, preferred_element_type=F32)
        if nb > 1:
            p_p = jnp.exp(s_p - m)
            den = den + jnp.sum(p_p, axis=-1, keepdims=True)
            acc = acc + lax.dot_general(p_p.astype(BF16), vp, pv, preferred_element_type=F32)
        out = acc / den
        lse = jnp.broadcast_to(m + jnp.log(den), out.shape)
        for u, (r, n) in enumerate(units):
            o_ref[0, h, sub_rows(r, n), :] = out[u]
            lse_ref[0, h, sub_rows(r, n), :] = lse[u]
        return carry

    lax.fori_loop(0, ATT_HEADS_PER_GROUP, head, 0)


def _attention_group(qkv, cosf, sinf, perm, gi, dilation):
    bsz, s, _ = qkv.shape
    gw = ATT_HEADS_PER_GROUP * ATT_HEAD_DIM
    col = lambda which: pl.BlockSpec((1, s, gw), lambda b: (b, 0, which * 3 + gi))
    tab = pl.BlockSpec((1, s, LANES), lambda b: (b, 0, 0))
    out_spec = pl.BlockSpec((1, ATT_HEADS_PER_GROUP, s, ATT_HEAD_DIM), lambda b: (b, 0, 0, 0))
    shape = jax.ShapeDtypeStruct((bsz, ATT_HEADS_PER_GROUP, s, ATT_HEAD_DIM), F32)
    scratch = pltpu.VMEM((ATT_HEADS_PER_GROUP, s, ATT_HEAD_DIM), F32)
    return pl.pallas_call(
        functools.partial(_attn_kernel, dilation=dilation),
        out_shape=(shape, shape),
        grid=(bsz,),
        in_specs=[col(0), col(1), col(2), tab, tab, pl.BlockSpec((LANES, LANES), lambda b: (0, 0))],
        out_specs=(out_spec, out_spec),
        scratch_shapes=[scratch, scratch, scratch],
        compiler_params=_cparams("parallel"),
        name="dilated_attention_%d" % dilation,
    )(qkv, qkv, qkv, cosf, sinf, perm)


def _merge_kernel(y_ref, o1_ref, o2_ref, o3_ref, l1_ref, l2_ref, l3_ref, ws_ref, wa_ref,
                  gs_ref, ga_ref, out_ref, att_ref):
    @pl.when(pl.program_id(1) == 0)
    def _():
        for h in range(ATT_HEADS_PER_GROUP):
            l1, l2, l3 = l1_ref[0, h], l2_ref[0, h], l3_ref[0, h]
            m = jnp.maximum(jnp.maximum(l1, l2), l3)
            e1, e2, e3 = jnp.exp(l1 - m), jnp.exp(l2 - m), jnp.exp(l3 - m)
            num = e1 * o1_ref[0, h] + e2 * o2_ref[0, h] + e3 * o3_ref[0, h]
            att_ref[:, h * ATT_HEAD_DIM:(h + 1) * ATT_HEAD_DIM] = (num / (e1 + e2 + e3)).astype(BF16)

    y_ssd = jnp.dot(y_ref[...], ws_ref[...], preferred_element_type=F32)
    y_att = jnp.dot(att_ref[...], wa_ref[...], preferred_element_type=F32)
    out_ref[...] = (gs_ref[...].astype(F32) * y_ssd
                    + ga_ref[...].astype(F32) * y_att).astype(out_ref.dtype)


def _merge(y_ssd, outs, lses, w_ssd_br, w_att_br, gates, s, tm=512, tn=1024):
    t = y_ssd.shape[0]
    nj = D_MODEL // tn
    per_batch = s // tm
    head_major = pl.BlockSpec((1, ATT_HEADS_PER_GROUP, tm, ATT_HEAD_DIM),
                              lambda i, j: (i // per_batch, 0, i % per_batch, 0))
    return pl.pallas_call(
        _merge_kernel,
        out_shape=jax.ShapeDtypeStruct((t, D_MODEL), BF16),
        grid=(t // tm, nj),
        in_specs=[pl.BlockSpec((tm, D_INNER), lambda i, j: (i, 0))] + [head_major] * 6
                 + [pl.BlockSpec((D_INNER, tn), lambda i, j: (0, j)),
                    pl.BlockSpec((ATT_OUT, tn), lambda i, j: (0, j)),
                    pl.BlockSpec((tm, tn), lambda i, j: (i, j)),
                    pl.BlockSpec((tm, tn), lambda i, j: (i, j + nj))],
        out_specs=pl.BlockSpec((tm, tn), lambda i, j: (i, j)),
        scratch_shapes=[pltpu.VMEM((tm, ATT_OUT), BF16)],
        compiler_params=_cparams("parallel", "arbitrary"),
        name="branch_merge",
    )(y_ssd, *outs, *lses, w_ssd_br, w_att_br, gates, gates)


def _layer_norm(v, g, b):
    mu = jnp.mean(v, axis=-1, keepdims=True)
    c = v - mu
    var = jnp.mean(c * c, axis=-1, keepdims=True)
    return c * lax.rsqrt(var + LN_EPS) * g + b


def _pack_rows(v):
    half = D_MODEL // 2

    def bits(a):
        return lax.bitcast_convert_type(a.astype(BF16).astype(F32), jnp.uint32)

    return [(bits(v[:, half + s * LANES:half + (s + 1) * LANES]) & jnp.uint32(0xFFFF0000))
            | (bits(v[:, s * LANES:(s + 1) * LANES]) >> 16) for s in range(ROW_TILE)]


def _unpack_pair(c):
    return (lax.bitcast_convert_type(c << 16, F32),
            lax.bitcast_convert_type(c & jnp.uint32(0xFFFF0000), F32))


def _unpack_rows(chunks):
    pairs = [_unpack_pair(c) for c in chunks]
    return (jnp.concatenate([p[0] for p in pairs], axis=1),
            jnp.concatenate([p[1] for p in pairs], axis=1))


def _load_token_tiles(ref, row0, n_tokens):
    return [ref[pl.ds(row0 + s, n_tokens, stride=ROW_TILE), :] for s in range(ROW_TILE)]


def _store_token_tiles(ref, chunks):
    n_tokens = chunks[0].shape[0]
    for s, chunk in enumerate(chunks):
        ref[pl.ds(s, n_tokens, stride=ROW_TILE), :] = chunk


def _proj_ln_kernel(m_ref, w_ref, x_ref, g_ref, b_ref, o_ref, op_ref):
    v = DEEPNORM_ALPHA * x_ref[...] + jnp.dot(m_ref[...], w_ref[...], preferred_element_type=F32)
    out = _layer_norm(v, g_ref[...], b_ref[...])
    o_ref[...] = out
    _store_token_tiles(op_ref, _pack_rows(out))


def _proj_ln(merged, w_o, x, g, b, tm=256):
    t = x.shape[0]
    row = lambda: pl.BlockSpec((tm, D_MODEL), lambda i: (i, 0))
    const = lambda shape: pl.BlockSpec(shape, lambda i: (0, 0))
    return pl.pallas_call(
        _proj_ln_kernel,
        out_shape=(jax.ShapeDtypeStruct((t, D_MODEL), F32),
                   jax.ShapeDtypeStruct((t * ROW_TILE, LANES), jnp.uint32)),
        grid=(t // tm,),
        in_specs=[row(), const((D_MODEL, D_MODEL)), row(), const((1, D_MODEL)), const((1, D_MODEL))],
        out_specs=(row(), pl.BlockSpec((tm * ROW_TILE, LANES), lambda i: (i, 0))),
        compiler_params=_cparams("parallel"),
        name="out_proj_layernorm",
    )(merged, w_o, x, g, b)


def _router_kernel(x_ref, w_ref, bias_ref, idx_ref, wt_ref, rank_ref, cnt_ref, carry_ref):
    tm = x_ref.shape[0]
    ne, ng, eg = N_EXPERTS, N_EXPERT_GROUPS, EXPERTS_PER_GROUP

    @pl.when(pl.program_id(0) == 0)
    def _():
        carry_ref[...] = jnp.zeros_like(carry_ref)

    logits = lax.dot_general(w_ref[...], x_ref[...], (((1,), (1,)), ((), ())),
                             precision=lax.Precision.HIGHEST, preferred_element_type=F32)
    scores = _sigmoid(logits)
    sel = scores + bias_ref[...]
    sub = lax.broadcasted_iota(jnp.int32, (eg, tm), 0)
    neg = -jnp.inf

    gscore = []
    for g in range(ng):
        sg = sel[g * eg:(g + 1) * eg, :]
        m1 = jnp.max(sg, axis=0, keepdims=True)
        first = jnp.min(jnp.where(sg == m1, sub, eg), axis=0, keepdims=True)
        m2 = jnp.max(jnp.where(sub == first, neg, sg), axis=0, keepdims=True)
        gscore.append(m1 + m2)
    keep = []
    for g in range(ng):
        beaten = jnp.zeros((1, tm), jnp.int32)
        for o in range(ng):
            if o == g:
                continue
            wins = (gscore[o] >= gscore[g]) if o < g else (gscore[o] > gscore[g])
            beaten = beaten + wins.astype(jnp.int32)
        keep.append(beaten < TOPK_GROUPS)
    masked = jnp.concatenate(
        [jnp.where(keep[g], sel[g * eg:(g + 1) * eg, :], neg) for g in range(ng)], axis=0)

    eidx = lax.broadcasted_iota(jnp.int32, (ne, tm), 0)
    picks, weights, chosen_masks = [], [], []
    for _ in range(TOP_K):
        mx = jnp.max(masked, axis=0, keepdims=True)
        pick = jnp.min(jnp.where(masked == mx, eidx, ne), axis=0, keepdims=True)
        chosen = eidx == pick
        weights.append(jnp.sum(jnp.where(chosen, scores, 0.0), axis=0, keepdims=True))
        picks.append(pick)
        chosen_masks.append(chosen)
        masked = jnp.where(chosen, neg, masked)
    w = jnp.concatenate(weights, axis=0)
    idx_ref[...] = jnp.concatenate(picks, axis=0)
    wt_ref[...] = w / jnp.sum(w, axis=0, keepdims=True) * ROUTED_SCALE

    cnt = jnp.zeros((ne, tm), F32)
    for chosen in chosen_masks:
        cnt = cnt + jnp.where(chosen, 1.0, 0.0)
    earlier = (lax.broadcasted_iota(jnp.int32, (tm, tm), 0)
               < lax.broadcasted_iota(jnp.int32, (tm, tm), 1))
    before = jnp.dot(cnt.astype(BF16), jnp.where(earlier, 1.0, 0.0).astype(BF16),
                     preferred_element_type=F32) + carry_ref[...]
    rank_ref[...] = jnp.concatenate(
        [jnp.sum(jnp.where(chosen, before, 0.0), axis=0, keepdims=True) for chosen in chosen_masks],
        axis=0).astype(jnp.int32)
    carry_ref[...] = carry_ref[...] + jnp.sum(cnt, axis=1, keepdims=True)
    cnt_ref[...] = carry_ref[...]


def _router(x, w_router_t, bias, tm=512):
    t = x.shape[0]
    tok = lambda: pl.BlockSpec((TOP_K, tm), lambda i: (0, i))
    return pl.pallas_call(
        _router_kernel,
        out_shape=(jax.ShapeDtypeStruct((TOP_K, t), jnp.int32),
                   jax.ShapeDtypeStruct((TOP_K, t), F32),
                   jax.ShapeDtypeStruct((TOP_K, t), jnp.int32),
                   jax.ShapeDtypeStruct((N_EXPERTS, 1), F32)),
        grid=(t // tm,),
        in_specs=[pl.BlockSpec((tm, D_MODEL), lambda i: (i, 0)),
                  pl.BlockSpec((N_EXPERTS, D_MODEL), lambda i: (0, 0)),
                  pl.BlockSpec((N_EXPERTS, 1), lambda i: (0, 0))],
        out_specs=(tok(), tok(), tok(), pl.BlockSpec((N_EXPERTS, 1), lambda i: (0, 0))),
        scratch_shapes=[pltpu.VMEM((N_EXPERTS, 1), F32)],
        compiler_params=_cparams("arbitrary"),
        name="router_topk",
    )(x, w_router_t, bias)


def _gather_tiles(src_hbm, idx_ref, dst_ref, sem, r0, r1):
    for r in range(r0, r1):
        first = pl.multiple_of(idx_ref[0, 0, r], ROW_TILE)
        pltpu.make_async_copy(src_hbm.at[pl.ds(first, ROW_TILE), :],
                              dst_ref.at[pl.ds(r * ROW_TILE, ROW_TILE), :], sem).start()


def _wait_tiles(src_hbm, dst_ref, sem, n_tokens):
    pltpu.make_async_copy(src_hbm.at[pl.ds(0, n_tokens * ROW_TILE), :], dst_ref, sem).wait()


def _experts_kernel(be_ref, na_ref, tok0_ref, tokn_ref, x_hbm, wrow_ref, wg_ref, wu_ref, wd_ref,
                    y_ref, buf_ref, sem_ref, wg16_ref, wu16_ref, wd16_ref):
    i = pl.program_id(0)
    n_active = na_ref[0]
    slot = lax.rem(i, 2)
    half = D_MODEL // 2

    @pl.when((i == 0) & (n_active > 0))
    def _():
        _gather_tiles(x_hbm, tok0_ref, buf_ref.at[0], sem_ref.at[0], 0, MOE_ROWS)

    new_expert = (i == 0) | (be_ref[i] != be_ref[jnp.maximum(i - 1, 0)])

    @pl.when(new_expert & (i < n_active))
    def _():
        wg16_ref[...] = wg_ref[0].astype(BF16)
        wu16_ref[...] = wu_ref[0].astype(BF16)
        wd16_ref[...] = wd_ref[0].astype(BF16)

    @pl.when(i < n_active)
    def _():
        cur, nxt = buf_ref.at[slot], buf_ref.at[1 - slot]
        _wait_tiles(x_hbm, cur, sem_ref.at[slot], MOE_ROWS)
        group = MOE_ROWS // ROW_TILE
        gate = up = None
        for pair in range(ROW_TILE // 2):
            lo, hi = [], []
            for s in (2 * pair, 2 * pair + 1):
                _gather_tiles(x_hbm, tokn_ref, nxt, sem_ref.at[1 - slot], s * group, (s + 1) * group)
                c_lo, c_hi = _unpack_pair(cur[pl.ds(s, MOE_ROWS, stride=ROW_TILE), :])
                lo.append(c_lo.astype(BF16))
                hi.append(c_hi.astype(BF16))
            x_lo, x_hi = jnp.concatenate(lo, axis=1), jnp.concatenate(hi, axis=1)
            k_lo = slice(2 * pair * LANES, (2 * pair + 2) * LANES)
            k_hi = slice(half + 2 * pair * LANES, half + (2 * pair + 2) * LANES)
            g = (jnp.dot(x_lo, wg16_ref[k_lo, :], preferred_element_type=F32)
                 + jnp.dot(x_hi, wg16_ref[k_hi, :], preferred_element_type=F32))
            u = (jnp.dot(x_lo, wu16_ref[k_lo, :], preferred_element_type=F32)
                 + jnp.dot(x_hi, wu16_ref[k_hi, :], preferred_element_type=F32))
            gate = g if gate is None else gate + g
            up = u if up is None else up + u
        h = gate * _sigmoid(gate) * up * wrow_ref[...]
        y = jnp.dot(h.astype(BF16), wd16_ref[...], preferred_element_type=F32)
        _store_token_tiles(y_ref, _pack_rows(y))

    @pl.when(i == n_active - 1)
    def _():
        _wait_tiles(x_hbm, buf_ref.at[1 - slot], sem_ref.at[1 - slot], MOE_ROWS)

    @pl.when(i >= n_active)
    def _():
        y_ref[...] = jnp.zeros_like(y_ref)


def _experts(xp, block_e, n_active, row_tok, row_w, w_gate, w_up, w_down, layer):
    n_blocks = row_tok.shape[0]
    n_rows = n_blocks * MOE_ROWS
    smem_blk = lambda index_map: pl.BlockSpec((1, 1, MOE_ROWS), index_map, memory_space=pltpu.SMEM)
    wspec = lambda a, b: pl.BlockSpec((1, 1, a, b), lambda i, be, na: (layer, be[i], 0, 0))

    def kernel(be_ref, na_ref, tok0_ref, tokn_ref, x_hbm, wrow_ref, wg_ref, wu_ref, wd_ref, *rest):
        _experts_kernel(be_ref, na_ref, tok0_ref, tokn_ref, x_hbm, wrow_ref, wg_ref.at[0],
                        wu_ref.at[0], wd_ref.at[0], *rest)

    grid_spec = pltpu.PrefetchScalarGridSpec(
        num_scalar_prefetch=2,
        grid=(n_blocks,),
        in_specs=[smem_blk(lambda i, be, na: (0, 0, 0)),
                  smem_blk(lambda i, be, na: (jnp.minimum(i + 1, n_blocks - 1), 0, 0)),
                  pl.BlockSpec(memory_space=pl.ANY),
                  pl.BlockSpec((MOE_ROWS, 1), lambda i, be, na: (i, 0)),
                  wspec(D_MODEL, D_EXPERT), wspec(D_MODEL, D_EXPERT), wspec(D_EXPERT, D_MODEL)],
        out_specs=pl.BlockSpec((MOE_ROWS * ROW_TILE, LANES), lambda i, be, na: (i, 0)),
        scratch_shapes=[pltpu.VMEM((2, MOE_ROWS * ROW_TILE, LANES), jnp.uint32),
                        pltpu.SemaphoreType.DMA((2,)),
                        pltpu.VMEM((D_MODEL, D_EXPERT), BF16),
                        pltpu.VMEM((D_MODEL, D_EXPERT), BF16),
                        pltpu.VMEM((D_EXPERT, D_MODEL), BF16)])
    return pl.pallas_call(
        kernel,
        out_shape=jax.ShapeDtypeStruct((n_rows * ROW_TILE, LANES), jnp.uint32),
        grid_spec=grid_spec,
        compiler_params=_cparams("arbitrary"),
        name="routed_experts",
    )(block_e, n_active, row_tok, row_tok, xp, row_w, w_gate, w_up, w_down)


def _combine_kernel(d0_ref, dn_ref, y_hbm, x_ref, wg_ref, wu_ref, wd_ref, g_ref, b_ref,
                    o_ref, o16_ref, buf_ref, sem_ref):
    i = pl.program_id(0)
    n = pl.num_programs(0)
    slot = lax.rem(i, 2)
    tm = COMBINE_TOKENS
    rows = tm * TOP_K

    @pl.when(i == 0)
    def _():
        _gather_tiles(y_hbm, d0_ref, buf_ref.at[0], sem_ref.at[0], 0, rows)

    _wait_tiles(y_hbm, buf_ref.at[slot], sem_ref.at[slot], rows)
    _gather_tiles(y_hbm, dn_ref, buf_ref.at[1 - slot], sem_ref.at[1 - slot], 0, rows)

    x = x_ref[...]
    x16 = x.astype(BF16)
    gate = jnp.dot(x16, wg_ref[...], preferred_element_type=F32)
    up = jnp.dot(x16, wu_ref[...], preferred_element_type=F32)
    h = (gate * _sigmoid(gate) * up).astype(BF16)
    v = DEEPNORM_ALPHA * x + jnp.dot(h, wd_ref[...], preferred_element_type=F32)

    lo = [None] * ROW_TILE
    hi = [None] * ROW_TILE
    for k in range(TOP_K):
        chunks = _load_token_tiles(buf_ref.at[slot], k * tm * ROW_TILE, tm)
        for s, c in enumerate(chunks):
            c_lo, c_hi = _unpack_pair(c)
            lo[s] = c_lo if k == 0 else lo[s] + c_lo
            hi[s] = c_hi if k == 0 else hi[s] + c_hi
    v = v + jnp.concatenate(lo + hi, axis=1)
    out = _layer_norm(v, g_ref[...], b_ref[...])
    o_ref[...] = out
    o16_ref[...] = out.astype(BF16)

    @pl.when(i == n - 1)
    def _():
        _wait_tiles(y_hbm, buf_ref.at[1 - slot], sem_ref.at[1 - slot], rows)


def _combine(y_rows, dest_blocks, x, w_gate_s, w_up_s, w_down_s, g, b):
    t = x.shape[0]
    tm = COMBINE_TOKENS
    rows = tm * TOP_K
    row = lambda: pl.BlockSpec((tm, D_MODEL), lambda i: (i, 0))
    const = lambda shape: pl.BlockSpec(shape, lambda i: (0, 0))
    nblk = t // tm
    smem_blk = lambda index_map: pl.BlockSpec((1, 1, rows), index_map, memory_space=pltpu.SMEM)
    return pl.pallas_call(
        _combine_kernel,
        out_shape=(jax.ShapeDtypeStruct((t, D_MODEL), F32), jax.ShapeDtypeStruct((t, D_MODEL), BF16)),
        grid=(nblk,),
        in_specs=[smem_blk(lambda i: (0, 0, 0)),
                  smem_blk(lambda i: (jnp.minimum(i + 1, nblk - 1), 0, 0)),
                  pl.BlockSpec(memory_space=pl.ANY),
                  row(),
                  const((D_MODEL, D_SHARED)), const((D_MODEL, D_SHARED)), const((D_SHARED, D_MODEL)),
                  const((1, D_MODEL)), const((1, D_MODEL))],
        out_specs=(row(), row()),
        scratch_shapes=[pltpu.VMEM((2, rows * ROW_TILE, LANES), jnp.uint32),
                        pltpu.SemaphoreType.DMA((2,))],
        compiler_params=_cparams("arbitrary"),
        name="moe_combine_layernorm",
    )(dest_blocks, dest_blocks, y_rows, x, w_gate_s, w_up_s, w_down_s, g, b)


def _dispatch_plan(idx_t, wt_t, rank_t, counts):
    t = idx_t.shape[1]
    tk = t * TOP_K
    n_blocks = tk // MOE_ROWS + N_EXPERTS
    experts = jnp.arange(N_EXPERTS, dtype=jnp.int32)
    counts = counts.reshape(N_EXPERTS).astype(jnp.int32)
    padded = (counts + MOE_ROWS - 1) // MOE_ROWS * MOE_ROWS
    pad_end = jnp.cumsum(padded)
    pad_start = pad_end - padded
    start = jnp.cumsum(counts) - counts
    first_row = jnp.sum(jnp.where(idx_t[None] == experts[:, None, None],
                                  pad_start[:, None, None], 0), axis=0)
    dest_t = (first_row + rank_t).astype(jnp.int32)
    tok = jnp.broadcast_to(jnp.arange(t, dtype=jnp.int32)[None, :], (TOP_K, t))
    _, sorted_tok, sorted_w = lax.sort((dest_t.reshape(-1), tok.reshape(-1), wt_t.reshape(-1)),
                                       num_keys=1)
    block_first = jnp.arange(n_blocks, dtype=jnp.int32) * MOE_ROWS
    block_e = jnp.minimum(jnp.sum(pad_end[None, :] <= block_first[:, None], axis=1),
                          N_EXPERTS - 1).astype(jnp.int32)
    n_active = (pad_end[-1] // MOE_ROWS).astype(jnp.int32).reshape(1)
    within = (block_first - pad_start[block_e])[:, None] + jnp.arange(MOE_ROWS, dtype=jnp.int32)[None, :]
    valid = (within < counts[block_e][:, None]) & (block_first < pad_end[-1])[:, None]
    pos = jnp.clip(start[block_e][:, None] + within, 0, tk - 1)
    row_tok = (jnp.where(valid, sorted_tok[pos], 0) * ROW_TILE).astype(jnp.int32)
    row_w = jnp.where(valid, sorted_w[pos], 0.0).astype(F32)
    return (block_e, n_active, row_tok.reshape(n_blocks, 1, MOE_ROWS),
            row_w.reshape(n_blocks * MOE_ROWS, 1), dest_t)


def _layer(x, x16, cosf, sinf, perm, expand, bsz, s, l, w_in, w_tail, w_gate_e, w_up_e, w_down_e, p):
    t = bsz * s
    zs = _matmul(x16, w_in, l, 0, D_INNER, p["zero_z"], "silu", BF16, 1024, 1024)
    xbc = _matmul(x16, w_in, l, D_INNER, CONV_DIM, p["zero_xbc"], "none", BF16, 1024, 1024)
    dt = _matmul(x16, w_in, l, DT_COL, HEAD_PAD, p["dt_bias"], "softplus", F32, 1024, HEAD_PAD)
    qkv = _matmul(x16, w_tail, l, 0, 3 * ATT_WIDTH, p["zero_qkv"], "none", BF16, 1024, 768)
    gates = _matmul(x16, w_tail, l, 3 * ATT_WIDTH, 2 * D_MODEL, p["gate_b"], "sigmoid", BF16, 1024, 512)

    y_ssd = _ssd(xbc.reshape(bsz, s, CONV_DIM), zs.reshape(bsz, s, D_INNER),
                 dt.reshape(bsz, s, HEAD_PAD), p["conv_w"], p["conv_b"], p["a_log"], p["d_skip"],
                 p["norm_w"], expand).reshape(t, D_INNER)

    qkv3 = qkv.reshape(bsz, s, 3 * ATT_WIDTH)
    outs, lses = [], []
    for gi, (_, dilation) in enumerate(ATT_GROUPS):
        o, lse = _attention_group(qkv3, cosf, sinf, perm, gi, dilation)
        outs.append(o)
        lses.append(lse)

    merged = _merge(y_ssd, outs, lses, p["w_ssd_br"], p["w_att_br"], gates, s)
    x1, x1p = _proj_ln(merged, p["w_o"], x, p["ln1_g"], p["ln1_b"])

    idx_t, wt_t, rank_t, counts = _router(x1, p["w_router_t"], p["router_bias"])
    block_e, n_active, row_tok, row_w, dest_t = _dispatch_plan(idx_t, wt_t, rank_t, counts)
    y_rows = _experts(x1p, block_e, n_active, row_tok, row_w, w_gate_e, w_up_e, w_down_e, l)
    nt = t // COMBINE_TOKENS
    dest_blocks = (dest_t * ROW_TILE).reshape(TOP_K, nt, COMBINE_TOKENS).transpose(1, 0, 2)
    dest_blocks = dest_blocks.reshape(nt, 1, TOP_K * COMBINE_TOKENS)
    return _combine(y_rows, dest_blocks, x1, p["w_gate_s"], p["w_up_s"], p["w_down_s"],
                    p["ln2_g"], p["ln2_b"])


def _pad_heads(v):
    return jnp.pad(v.astype(F32), (0, HEAD_PAD - SSD_HEADS)).reshape(1, HEAD_PAD)


def _layer_params(l, conv_w, conv_b, dt_bias, a_log, d_skip, ssd_norm_w, w_ssd_br, w_att_br,
                  gate_b, w_o, ln1_g, ln1_b, w_router, router_bias, w_gate_s, w_up_s, w_down_s,
                  ln2_g, ln2_b):
    row = lambda v: v.astype(F32).reshape(1, -1)
    return dict(
        zero_z=jnp.zeros((1, D_INNER), F32), zero_xbc=jnp.zeros((1, CONV_DIM), F32),
        zero_qkv=jnp.zeros((1, 3 * ATT_WIDTH), F32),
        dt_bias=_pad_heads(dt_bias[l]), gate_b=row(gate_b[l]),
        conv_w=conv_w[l].astype(F32), conv_b=row(conv_b[l]),
        a_log=_pad_heads(a_log[l]), d_skip=_pad_heads(d_skip[l]), norm_w=row(ssd_norm_w[l]),
        w_ssd_br=w_ssd_br[l].astype(BF16), w_att_br=w_att_br[l].astype(BF16),
        w_o=w_o[l].astype(BF16), ln1_g=row(ln1_g[l]), ln1_b=row(ln1_b[l]),
        w_router_t=w_router[l].astype(F32).T, router_bias=router_bias[l].astype(F32).reshape(-1, 1),
        w_gate_s=w_gate_s[l].astype(BF16), w_up_s=w_up_s[l].astype(BF16),
        w_down_s=w_down_s[l].astype(BF16), ln2_g=row(ln2_g[l]), ln2_b=row(ln2_b[l]))


def _rope_tables(positions):
    half = ROPE_DIM // 2
    inv_freq = ROPE_THETA ** (-jnp.arange(0, ROPE_DIM, 2, dtype=F32) / ROPE_DIM)
    ang = positions.astype(F32)[..., None] * inv_freq
    cos, sin = jnp.cos(ang), jnp.sin(ang)
    pad = positions.shape + (ATT_HEAD_DIM - ROPE_DIM,)
    cosf = jnp.concatenate([cos, cos, jnp.ones(pad, F32)], axis=-1)
    sinf = jnp.concatenate([-sin, sin, jnp.zeros(pad, F32)], axis=-1)
    k = jnp.arange(ATT_HEAD_DIM)[:, None]
    j = jnp.arange(ATT_HEAD_DIM)[None, :]
    perm = ((j < half) & (k == j + half)) | ((j >= half) & (j < ROPE_DIM) & (k == j - half))
    return cosf, sinf, perm.astype(BF16)


def kernel(x, positions, w_in, conv_w, conv_b, dt_bias, a_log, d_skip, ssd_norm_w, w_ssd_br,
           w_att_br, gate_b, w_o, ln1_g, ln1_b, w_router, router_bias, w_gate_e, w_up_e, w_down_e,
           w_gate_s, w_up_s, w_down_s, ln2_g, ln2_b):
    bsz, s, d = x.shape
    cosf, sinf, perm = _rope_tables(positions)
    head_of_col = jnp.arange(D_INNER) // SSD_HEAD_DIM
    expand = (jnp.arange(HEAD_PAD)[:, None] == head_of_col[None, :]).astype(BF16)
    w_in = w_in.astype(F32)
    w_tail = w_in[:, :, TAIL_COL:].astype(BF16)
    w_gate_e, w_up_e, w_down_e = (w.astype(F32) for w in (w_gate_e, w_up_e, w_down_e))
    xf = x.reshape(bsz * s, d).astype(F32)
    x16 = xf.astype(BF16)
    for l in range(DEPTH):
        p = _layer_params(l, conv_w, conv_b, dt_bias, a_log, d_skip, ssd_norm_w, w_ssd_br, w_att_br,
                          gate_b, w_o, ln1_g, ln1_b, w_router, router_bias, w_gate_s, w_up_s,
                          w_down_s, ln2_g, ln2_b)
        xf, x16 = _layer(xf, x16, cosf, sinf, perm, expand, bsz, s, l, w_in, w_tail,
                         w_gate_e, w_up_e, w_down_e, p)
    return xf.reshape(bsz, s, d).astype(x.dtype)
```

```python
import functools

import jax
import jax.numpy as jnp
from jax import lax
from jax.experimental import pallas as pl
from jax.experimental.pallas import tpu as pltpu

D_MODEL = 2048
DEPTH = 2

D_INNER = 4096
SSD_HEAD_DIM = 64
SSD_HEADS = 64
SSD_GROUPS = 8
SSD_HEADS_PER_GROUP = 8
SSD_STATE = 128
SSD_CONV = 4
SSD_CHUNK = 128
CONV_DIM = D_INNER + 2 * SSD_GROUPS * SSD_STATE
RMS_EPS = 1e-5

ATT_HEAD_DIM = 128
ATT_GROUPS = ((128, 1), (512, 4), (2048, 16))
ATT_HEADS_PER_GROUP = 4
ATT_WIDTH = 1536
ATT_OUT = 512
ATT_BLOCK = 128
ROPE_THETA = 500000.0
ROPE_DIM = 32

IN_SIZES = (D_INNER, CONV_DIM, SSD_HEADS, ATT_WIDTH, ATT_WIDTH, ATT_WIDTH, D_MODEL, D_MODEL)
DT_COL = D_INNER + CONV_DIM
TAIL_COL = DT_COL + SSD_HEADS

N_EXPERTS = 64
N_EXPERT_GROUPS = 8
EXPERTS_PER_GROUP = 8
TOPK_GROUPS = 4
TOP_K = 8
D_EXPERT = 512
D_SHARED = 512
ROUTED_SCALE = 2.5

DEEPNORM_ALPHA = (2 * DEPTH) ** 0.25
LN_EPS = 1e-5

LANES = 128
HEAD_PAD = 128
ROW_TILE = 8
MOE_ROWS = 256
COMBINE_TOKENS = 128
VMEM_LIMIT = 56 * 1024 * 1024

F32 = jnp.float32
BF16 = jnp.bfloat16


def _cparams(*sem):
    return pltpu.CompilerParams(dimension_semantics=sem, vmem_limit_bytes=VMEM_LIMIT)


def _sigmoid(v):
    return 1.0 / (1.0 + jnp.exp(-v))


def _mm_kernel(x_ref, w_ref, b_ref, o_ref, *scratch, act):
    if scratch:
        w16_ref, = scratch

        @pl.when(pl.program_id(1) == 0)
        def _():
            w16_ref[...] = w_ref[0].astype(BF16)
        w = w16_ref[...]
    else:
        w = w_ref[0]

    acc = jnp.dot(x_ref[...], w, preferred_element_type=F32) + b_ref[...]
    if act == "silu":
        acc = acc * _sigmoid(acc)
    elif act == "sigmoid":
        acc = _sigmoid(acc)
    elif act == "softplus":
        acc = jnp.maximum(acc, 0.0) + jnp.log(1.0 + jnp.exp(-jnp.abs(acc)))
    o_ref[...] = acc.astype(o_ref.dtype)


def _matmul(x, w, layer, col0, n, bias, act, out_dtype, tm, tn):
    m, k = x.shape
    assert col0 % tn == 0 and n % tn == 0 and m % tm == 0
    j0 = col0 // tn
    scratch = [] if w.dtype == BF16 else [pltpu.VMEM((k, tn), BF16)]
    return pl.pallas_call(
        functools.partial(_mm_kernel, act=act),
        out_shape=jax.ShapeDtypeStruct((m, n), out_dtype),
        grid=(n // tn, m // tm),
        in_specs=[pl.BlockSpec((tm, k), lambda j, i: (i, 0)),
                  pl.BlockSpec((1, k, tn), lambda j, i: (layer, 0, j0 + j)),
                  pl.BlockSpec((1, tn), lambda j, i: (0, j))],
        out_specs=pl.BlockSpec((tm, tn), lambda j, i: (i, j)),
        scratch_shapes=scratch,
        compiler_params=_cparams("parallel", "arbitrary"),
        name="proj_" + act,
    )(x, w, bias)


def _split_dot(v, e):
    hi = v.astype(BF16)
    lo = (v - hi.astype(F32)).astype(BF16)
    return (jnp.dot(hi, e, preferred_element_type=F32)
            + jnp.dot(lo, e, preferred_element_type=F32))


def _split_dot_left(e, v):
    hi = v.astype(BF16)
    lo = (v - hi.astype(F32)).astype(BF16)
    return (jnp.dot(e, hi, preferred_element_type=F32)
            + jnp.dot(e, lo, preferred_element_type=F32))


def _ssd_kernel(xbc_ref, zs_ref, dt_ref, convw_ref, convb_ref, alog_ref, dskip_ref, normw_ref,
                expand_ref, o_ref, xpad_ref, state_ref, y_ref, dtx_ref, dtex_ref, eax_ref):
    q = SSD_CHUNK
    gw = SSD_HEADS_PER_GROUP * SSD_HEAD_DIM

    @pl.when(pl.program_id(1) == 0)
    def _():
        xpad_ref[0:8, :] = jnp.zeros((8, CONV_DIM), F32)
        state_ref[...] = jnp.zeros_like(state_ref)

    xpad_ref[8:8 + q, :] = xbc_ref[0].astype(F32)

    def conv(c0, width):
        acc = convb_ref[:, c0:c0 + width]
        for j in range(SSD_CONV):
            lo = 8 - (SSD_CONV - 1) + j
            acc = acc + convw_ref[j:j + 1, c0:c0 + width] * xpad_ref[lo:lo + q, c0:c0 + width]
        return acc * _sigmoid(acc)

    row = lax.broadcasted_iota(jnp.int32, (q, q), 0)
    col = lax.broadcasted_iota(jnp.int32, (q, q), 1)
    causal = col <= row
    tril = jnp.where(causal, 1.0, 0.0).astype(BF16)

    dt = dt_ref[0]
    a = -jnp.exp(alog_ref[...])
    da = dt * a
    a_cs = _split_dot_left(tril, da)
    a_cs_t = a_cs.T
    a_last = a_cs[q - 1:q, :]
    expand = expand_ref[...]
    dtx_ref[...] = _split_dot(dt, expand)
    dtex_ref[...] = _split_dot(dt * jnp.exp(a_last - a_cs), expand)
    eax_ref[...] = _split_dot(jnp.exp(a_cs), expand)
    decx = _split_dot(jnp.broadcast_to(jnp.exp(a_last), (8, HEAD_PAD)), expand)[0:1, :]
    dskipx = _split_dot(jnp.broadcast_to(dskip_ref[...], (8, HEAD_PAD)), expand)[0:1, :]

    for g in range(SSD_GROUPS):
        c0 = g * gw
        xs = conv(c0, gw)
        bm = conv(D_INNER + g * SSD_STATE, SSD_STATE)
        cm = conv(D_INNER + SSD_GROUPS * SSD_STATE + g * SSD_STATE, SSD_STATE)
        bm_t = bm.T.astype(BF16)
        cm16 = cm.astype(BF16)
        cb = jnp.dot(cm16, bm_t, preferred_element_type=F32)
        xdt = (xs * dtx_ref[:, c0:c0 + gw]).astype(BF16)
        for r in range(SSD_HEADS_PER_GROUP):
            h = g * SSD_HEADS_PER_GROUP + r
            seg = a_cs[:, h:h + 1] - a_cs_t[h:h + 1, :]
            decay = jnp.exp(jnp.where(causal, seg, -jnp.inf))
            m = (cb * decay).astype(BF16)
            y_ref[:, h * SSD_HEAD_DIM:(h + 1) * SSD_HEAD_DIM] = jnp.dot(
                m, xdt[:, r * SSD_HEAD_DIM:(r + 1) * SSD_HEAD_DIM], preferred_element_type=F32)
        st = state_ref[:, c0:c0 + gw]
        y_off = jnp.dot(cm16, st.astype(BF16), preferred_element_type=F32) * eax_ref[:, c0:c0 + gw]
        xdte = (xs * dtex_ref[:, c0:c0 + gw]).astype(BF16)
        state_ref[:, c0:c0 + gw] = decx[:, c0:c0 + gw] * st + jnp.dot(
            bm_t, xdte, preferred_element_type=F32)
        y = y_ref[:, c0:c0 + gw] + y_off + dskipx[:, c0:c0 + gw] * xs
        y = y * zs_ref[0, :, c0:c0 + gw].astype(F32)
        ms = jnp.mean(y * y, axis=-1, keepdims=True)
        o_ref[0, :, c0:c0 + gw] = (y * lax.rsqrt(ms + RMS_EPS)
                                   * normw_ref[:, c0:c0 + gw]).astype(o_ref.dtype)

    xpad_ref[0:8, :] = xpad_ref[q:q + 8, :]


def _ssd(xbc, zs, dt, conv_w, conv_b, a_log, d_skip, norm_w, expand):
    bsz, s, _ = xbc.shape
    nc = s // SSD_CHUNK
    full = lambda shape: pl.BlockSpec(shape, lambda b, c: (0, 0))
    return pl.pallas_call(
        _ssd_kernel,
        out_shape=jax.ShapeDtypeStruct((bsz, s, D_INNER), BF16),
        grid=(bsz, nc),
        in_specs=[pl.BlockSpec((1, SSD_CHUNK, CONV_DIM), lambda b, c: (b, c, 0)),
                  pl.BlockSpec((1, SSD_CHUNK, D_INNER), lambda b, c: (b, c, 0)),
                  pl.BlockSpec((1, SSD_CHUNK, HEAD_PAD), lambda b, c: (b, c, 0)),
                  full((SSD_CONV, CONV_DIM)), full((1, CONV_DIM)), full((1, HEAD_PAD)),
                  full((1, HEAD_PAD)), full((1, D_INNER)), full((HEAD_PAD, D_INNER))],
        out_specs=pl.BlockSpec((1, SSD_CHUNK, D_INNER), lambda b, c: (b, c, 0)),
        scratch_shapes=[pltpu.VMEM((SSD_CHUNK + 8, CONV_DIM), F32),
                        pltpu.VMEM((SSD_STATE, D_INNER), F32),
                        pltpu.VMEM((SSD_CHUNK, D_INNER), F32),
                        pltpu.VMEM((SSD_CHUNK, D_INNER), F32),
                        pltpu.VMEM((SSD_CHUNK, D_INNER), F32),
                        pltpu.VMEM((SSD_CHUNK, D_INNER), F32)],
        compiler_params=_cparams("parallel", "arbitrary"),
        name="ssd_mixer",
    )(xbc, zs, dt, conv_w, conv_b, a_log, d_skip, norm_w, expand)


ATT_STAGE_ROWS = 256


def _attn_kernel(q_ref, k_ref, v_ref, cos_ref, sin_ref, perm_ref, o_ref, lse_ref,
                 q32_ref, k32_ref, v32_ref, *, dilation):
    s = q_ref.shape[1]
    blk, dh, d = ATT_BLOCK, ATT_HEAD_DIM, dilation
    nb = s // d // blk
    perm = perm_ref[...]

    def stage(c, carry):
        rows = pl.ds(pl.multiple_of(c * ATT_STAGE_ROWS, ATT_STAGE_ROWS), ATT_STAGE_ROWS)
        cos, sin = cos_ref[0, rows, :], sin_ref[0, rows, :]
        for h in range(ATT_HEADS_PER_GROUP):
            sl = slice(h * dh, (h + 1) * dh)
            qh, kh = q_ref[0, rows, sl], k_ref[0, rows, sl]
            q_rot = qh.astype(F32) * cos + jnp.dot(qh, perm, preferred_element_type=F32) * sin
            q32_ref[h, rows, :] = q_rot * (dh ** -0.5)
            k32_ref[h, rows, :] = (kh.astype(F32) * cos
                                   + jnp.dot(kh, perm, preferred_element_type=F32) * sin)
            v32_ref[h, rows, :] = v_ref[0, rows, sl].astype(F32)
        return carry

    lax.fori_loop(0, s // ATT_STAGE_ROWS, stage, 0)

    qi = lax.broadcasted_iota(jnp.int32, (blk, blk), 0)
    kj = lax.broadcasted_iota(jnp.int32, (blk, blk), 1)
    mask_c = kj <= qi
    mask_p = kj >= qi
    qk = (((2,), (2,)), ((0,), (0,)))
    pv = (((2,), (1,)), ((0,), (0,)))

    units = [(r, n) for r in range(d) for n in range(nb)]

    def sub_rows(r, n):
        start = r + d * blk * n
        return pl.ds(start, blk, stride=d) if d > 1 else pl.ds(start, blk)

    def gather(ref, h, shift):
        return jnp.stack([ref[h, sub_rows(r, max(n - shift, 0)), :] for r, n in units]).astype(BF16)

    def head(h, carry):
        qb = gather(q32_ref, h, 0)
        kc, vc = gather(k32_ref, h, 0), gather(v32_ref, h, 0)
        s_c = jnp.where(mask_c[None], lax.dot_general(qb, kc, qk, preferred_element_type=F32),
                        -jnp.inf)
        m = jnp.max(s_c, axis=-1, keepdims=True)
        if nb > 1:
            kp, vp = gather(k32_ref, h, 1), gather(v32_ref, h, 1)
            has_prev = jnp.stack([jnp.full((blk, blk), n > 0) for _, n in units])
            s_p = lax.dot_general(qb, kp, qk, preferred_element_type=F32)
            s_p = jnp.where(has_prev, jnp.where(mask_p[None], s_p, -jnp.inf), -jnp.inf)
            m = jnp.maximum(m, jnp.max(s_p, axis=-1, keepdims=True))
        p_c = jnp.exp(s_c - m)
        den = jnp.sum(p_c, axis=-1, keepdims=True)
        acc = lax.dot_general(p_c.astype(BF16), vc, pv, preferred_element_type=F32)
        if nb > 1:
            p_p = jnp.exp(s_p - m)
            den = den + jnp.sum(p_p, axis=-1, keepdims=True)
            acc = acc + lax.dot_general(p_p.astype(BF16), vp, pv, preferred_element_type=F32)
        out = acc / den
        lse = jnp.broadcast_to(m + jnp.log(den), out.shape)
        for u, (r, n) in enumerate(units):
            o_ref[0, h, sub_rows(r, n), :] = out[u]
            lse_ref[0, h, sub_rows(r, n), :] = lse[u]
        return carry

    lax.fori_loop(0, ATT_HEADS_PER_GROUP, head, 0)


def _attention_group(qkv, cosf, sinf, perm, gi, dilation):
    bsz, s, _ = qkv.shape
    gw = ATT_HEADS_PER_GROUP * ATT_HEAD_DIM
    col = lambda which: pl.BlockSpec((1, s, gw), lambda b: (b, 0, which * 3 + gi))
    tab = pl.BlockSpec((1, s, LANES), lambda b: (b, 0, 0))
    out_spec = pl.BlockSpec((1, ATT_HEADS_PER_GROUP, s, ATT_HEAD_DIM), lambda b: (b, 0, 0, 0))
    shape = jax.ShapeDtypeStruct((bsz, ATT_HEADS_PER_GROUP, s, ATT_HEAD_DIM), F32)
    scratch = pltpu.VMEM((ATT_HEADS_PER_GROUP, s, ATT_HEAD_DIM), F32)
    return pl.pallas_call(
        functools.partial(_attn_kernel, dilation=dilation),
        out_shape=(shape, shape),
        grid=(bsz,),
        in_specs=[col(0), col(1), col(2), tab, tab, pl.BlockSpec((LANES, LANES), lambda b: (0, 0))],
        out_specs=(out_spec, out_spec),
        scratch_shapes=[scratch, scratch, scratch],
        compiler_params=_cparams("parallel"),
        name="dilated_attention_%d" % dilation,
    )(qkv, qkv, qkv, cosf, sinf, perm)


def _merge_kernel(y_ref, o1_ref, o2_ref, o3_ref, l1_ref, l2_ref, l3_ref, ws_ref, wa_ref,
                  gs_ref, ga_ref, out_ref, att_ref):
    @pl.when(pl.program_id(1) == 0)
    def _():
        for h in range(ATT_HEADS_PER_GROUP):
            l1, l2, l3 = l1_ref[0, h], l2_ref[0, h], l3_ref[0, h]
            m = jnp.maximum(jnp.maximum(l1, l2), l3)
            e1, e2, e3 = jnp.exp(l1 - m), jnp.exp(l2 - m), jnp.exp(l3 - m)
            num = e1 * o1_ref[0, h] + e2 * o2_ref[0, h] + e3 * o3_ref[0, h]
            att_ref[:, h * ATT_HEAD_DIM:(h + 1) * ATT_HEAD_DIM] = (num / (e1 + e2 + e3)).astype(BF16)

    y_ssd = jnp.dot(y_ref[...], ws_ref[...], preferred_element_type=F32)
    y_att = jnp.dot(att_ref[...], wa_ref[...], preferred_element_type=F32)
    out_ref[...] = (gs_ref[...].astype(F32) * y_ssd
                    + ga_ref[...].astype(F32) * y_att).astype(out_ref.dtype)


def _merge(y_ssd, outs, lses, w_ssd_br, w_att_br, gates, s, tm=512, tn=1024):
    t = y_ssd.shape[0]
    nj = D_MODEL // tn
    per_batch = s // tm
    head_major = pl.BlockSpec((1, ATT_HEADS_PER_GROUP, tm, ATT_HEAD_DIM),
                              lambda i, j: (i // per_batch, 0, i % per_batch, 0))
    return pl.pallas_call(
        _merge_kernel,
        out_shape=jax.ShapeDtypeStruct((t, D_MODEL), BF16),
        grid=(t // tm, nj),
        in_specs=[pl.BlockSpec((tm, D_INNER), lambda i, j: (i, 0))] + [head_major] * 6
                 + [pl.BlockSpec((D_INNER, tn), lambda i, j: (0, j)),
                    pl.BlockSpec((ATT_OUT, tn), lambda i, j: (0, j)),
                    pl.BlockSpec((tm, tn), lambda i, j: (i, j)),
                    pl.BlockSpec((tm, tn), lambda i, j: (i, j + nj))],
        out_specs=pl.BlockSpec((tm, tn), lambda i, j: (i, j)),
        scratch_shapes=[pltpu.VMEM((tm, ATT_OUT), BF16)],
        compiler_params=_cparams("parallel", "arbitrary"),
        name="branch_merge",
    )(y_ssd, *outs, *lses, w_ssd_br, w_att_br, gates, gates)


def _layer_norm(v, g, b):
    mu = jnp.mean(v, axis=-1, keepdims=True)
    c = v - mu
    var = jnp.mean(c * c, axis=-1, keepdims=True)
    return c * lax.rsqrt(var + LN_EPS) * g + b


def _pack_rows(v):
    half = D_MODEL // 2

    def bits(a):
        return lax.bitcast_convert_type(a.astype(BF16).astype(F32), jnp.uint32)

    return [(bits(v[:, half + s * LANES:half + (s + 1) * LANES]) & jnp.uint32(0xFFFF0000))
            | (bits(v[:, s * LANES:(s + 1) * LANES]) >> 16) for s in range(ROW_TILE)]


def _unpack_pair(c):
    return (lax.bitcast_convert_type(c << 16, F32),
            lax.bitcast_convert_type(c & jnp.uint32(0xFFFF0000), F32))


def _unpack_rows(chunks):
    pairs = [_unpack_pair(c) for c in chunks]
    return (jnp.concatenate([p[0] for p in pairs], axis=1),
            jnp.concatenate([p[1] for p in pairs], axis=1))


def _load_token_tiles(ref, row0, n_tokens):
    return [ref[pl.ds(row0 + s, n_tokens, stride=ROW_TILE), :] for s in range(ROW_TILE)]


def _store_token_tiles(ref, chunks):
    n_tokens = chunks[0].shape[0]
    for s, chunk in enumerate(chunks):
        ref[pl.ds(s, n_tokens, stride=ROW_TILE), :] = chunk


def _proj_ln_kernel(m_ref, w_ref, x_ref, g_ref, b_ref, o_ref, op_ref):
    v = DEEPNORM_ALPHA * x_ref[...] + jnp.dot(m_ref[...], w_ref[...], preferred_element_type=F32)
    out = _layer_norm(v, g_ref[...], b_ref[...])
    o_ref[...] = out
    _store_token_tiles(op_ref, _pack_rows(out))


def _proj_ln(merged, w_o, x, g, b, tm=256):
    t = x.shape[0]
    row = lambda: pl.BlockSpec((tm, D_MODEL), lambda i: (i, 0))
    const = lambda shape: pl.BlockSpec(shape, lambda i: (0, 0))
    return pl.pallas_call(
        _proj_ln_kernel,
        out_shape=(jax.ShapeDtypeStruct((t, D_MODEL), F32),
                   jax.ShapeDtypeStruct((t * ROW_TILE, LANES), jnp.uint32)),
        grid=(t // tm,),
        in_specs=[row(), const((D_MODEL, D_MODEL)), row(), const((1, D_MODEL)), const((1, D_MODEL))],
        out_specs=(row(), pl.BlockSpec((tm * ROW_TILE, LANES), lambda i: (i, 0))),
        compiler_params=_cparams("parallel"),
        name="out_proj_layernorm",
    )(merged, w_o, x, g, b)


def _router_kernel(x_ref, w_ref, bias_ref, idx_ref, wt_ref, rank_ref, cnt_ref, carry_ref):
    tm = x_ref.shape[0]
    ne, ng, eg = N_EXPERTS, N_EXPERT_GROUPS, EXPERTS_PER_GROUP

    @pl.when(pl.program_id(0) == 0)
    def _():
        carry_ref[...] = jnp.zeros_like(carry_ref)

    logits = lax.dot_general(w_ref[...], x_ref[...], (((1,), (1,)), ((), ())),
                             precision=lax.Precision.HIGHEST, preferred_element_type=F32)
    scores = _sigmoid(logits)
    sel = scores + bias_ref[...]
    sub = lax.broadcasted_iota(jnp.int32, (eg, tm), 0)
    neg = -jnp.inf

    gscore = []
    for g in range(ng):
        sg = sel[g * eg:(g + 1) * eg, :]
        m1 = jnp.max(sg, axis=0, keepdims=True)
        first = jnp.min(jnp.where(sg == m1, sub, eg), axis=0, keepdims=True)
        m2 = jnp.max(jnp.where(sub == first, neg, sg), axis=0, keepdims=True)
        gscore.append(m1 + m2)
    keep = []
    for g in range(ng):
        beaten = jnp.zeros((1, tm), jnp.int32)
        for o in range(ng):
            if o == g:
                continue
            wins = (gscore[o] >= gscore[g]) if o < g else (gscore[o] > gscore[g])
            beaten = beaten + wins.astype(jnp.int32)
        keep.append(beaten < TOPK_GROUPS)
    masked = jnp.concatenate(
        [jnp.where(keep[g], sel[g * eg:(g + 1) * eg, :], neg) for g in range(ng)], axis=0)

    eidx = lax.broadcasted_iota(jnp.int32, (ne, tm), 0)
    picks, weights, chosen_masks = [], [], []
    for _ in range(TOP_K):
        mx = jnp.max(masked, axis=0, keepdims=True)
        pick = jnp.min(jnp.where(masked == mx, eidx, ne), axis=0, keepdims=True)
        chosen = eidx == pick
        weights.append(jnp.sum(jnp.where(chosen, scores, 0.0), axis=0, keepdims=True))
        picks.append(pick)
        chosen_masks.append(chosen)
        masked = jnp.where(chosen, neg, masked)
    w = jnp.concatenate(weights, axis=0)
    idx_ref[...] = jnp.concatenate(picks, axis=0)
    wt_ref[...] = w / jnp.sum(w, axis=0, keepdims=True) * ROUTED_SCALE

    cnt = jnp.zeros((ne, tm), F32)
    for chosen in chosen_masks:
        cnt = cnt + jnp.where(chosen, 1.0, 0.0)
    earlier = (lax.broadcasted_iota(jnp.int32, (tm, tm), 0)
               < lax.broadcasted_iota(jnp.int32, (tm, tm), 1))
    before = jnp.dot(cnt.astype(BF16), jnp.where(earlier, 1.0, 0.0).astype(BF16),
                     preferred_element_type=F32) + carry_ref[...]
    rank_ref[...] = jnp.concatenate(
        [jnp.sum(jnp.where(chosen, before, 0.0), axis=0, keepdims=True) for chosen in chosen_masks],
        axis=0).astype(jnp.int32)
    carry_ref[...] = carry_ref[...] + jnp.sum(cnt, axis=1, keepdims=True)
    cnt_ref[...] = carry_ref[...]


def _router(x, w_router_t, bias, tm=512):
    t = x.shape[0]
    tok = lambda: pl.BlockSpec((TOP_K, tm), lambda i: (0, i))
    return pl.pallas_call(
        _router_kernel,
        out_shape=(jax.ShapeDtypeStruct((TOP_K, t), jnp.int32),
                   jax.ShapeDtypeStruct((TOP_K, t), F32),
                   jax.ShapeDtypeStruct((TOP_K, t), jnp.int32),
                   jax.ShapeDtypeStruct((N_EXPERTS, 1), F32)),
        grid=(t // tm,),
        in_specs=[pl.BlockSpec((tm, D_MODEL), lambda i: (i, 0)),
                  pl.BlockSpec((N_EXPERTS, D_MODEL), lambda i: (0, 0)),
                  pl.BlockSpec((N_EXPERTS, 1), lambda i: (0, 0))],
        out_specs=(tok(), tok(), tok(), pl.BlockSpec((N_EXPERTS, 1), lambda i: (0, 0))),
        scratch_shapes=[pltpu.VMEM((N_EXPERTS, 1), F32)],
        compiler_params=_cparams("arbitrary"),
        name="router_topk",
    )(x, w_router_t, bias)


def _gather_tiles(src_hbm, idx_ref, dst_ref, sem, r0, r1):
    for r in range(r0, r1):
        first = pl.multiple_of(idx_ref[0, 0, r], ROW_TILE)
        pltpu.make_async_copy(src_hbm.at[pl.ds(first, ROW_TILE), :],
                              dst_ref.at[pl.ds(r * ROW_TILE, ROW_TILE), :], sem).start()


def _wait_tiles(src_hbm, dst_ref, sem, n_tokens):
    pltpu.make_async_copy(src_hbm.at[pl.ds(0, n_tokens * ROW_TILE), :], dst_ref, sem).wait()


def _experts_kernel(be_ref, na_ref, tok0_ref, tokn_ref, x_hbm, wrow_ref, wg_ref, wu_ref, wd_ref,
                    y_ref, buf_ref, sem_ref, wg16_ref, wu16_ref, wd16_ref):
    i = pl.program_id(0)
    n_active = na_ref[0]
    slot = lax.rem(i, 2)
    half = D_MODEL // 2

    @pl.when((i == 0) & (n_active > 0))
    def _():
        _gather_tiles(x_hbm, tok0_ref, buf_ref.at[0], sem_ref.at[0], 0, MOE_ROWS)

    new_expert = (i == 0) | (be_ref[i] != be_ref[jnp.maximum(i - 1, 0)])

    @pl.when(new_expert & (i < n_active))
    def _():
        wg16_ref[...] = wg_ref[0].astype(BF16)
        wu16_ref[...] = wu_ref[0].astype(BF16)
        wd16_ref[...] = wd_ref[0].astype(BF16)

    @pl.when(i < n_active)
    def _():
        cur, nxt = buf_ref.at[slot], buf_ref.at[1 - slot]
        _wait_tiles(x_hbm, cur, sem_ref.at[slot], MOE_ROWS)
        group = MOE_ROWS // ROW_TILE
        gate = up = None
        for pair in range(ROW_TILE // 2):
            lo, hi = [], []
            for s in (2 * pair, 2 * pair + 1):
                _gather_tiles(x_hbm, tokn_ref, nxt, sem_ref.at[1 - slot], s * group, (s + 1) * group)
                c_lo, c_hi = _unpack_pair(cur[pl.ds(s, MOE_ROWS, stride=ROW_TILE), :])
                lo.append(c_lo.astype(BF16))
                hi.append(c_hi.astype(BF16))
            x_lo, x_hi = jnp.concatenate(lo, axis=1), jnp.concatenate(hi, axis=1)
            k_lo = slice(2 * pair * LANES, (2 * pair + 2) * LANES)
            k_hi = slice(half + 2 * pair * LANES, half + (2 * pair + 2) * LANES)
            g = (jnp.dot(x_lo, wg16_ref[k_lo, :], preferred_element_type=F32)
                 + jnp.dot(x_hi, wg16_ref[k_hi, :], preferred_element_type=F32))
            u = (jnp.dot(x_lo, wu16_ref[k_lo, :], preferred_element_type=F32)
                 + jnp.dot(x_hi, wu16_ref[k_hi, :], preferred_element_type=F32))
            gate = g if gate is None else gate + g
            up = u if up is None else up + u
        h = gate * _sigmoid(gate) * up * wrow_ref[...]
        y = jnp.dot(h.astype(BF16), wd16_ref[...], preferred_element_type=F32)
        _store_token_tiles(y_ref, _pack_rows(y))

    @pl.when(i == n_active - 1)
    def _():
        _wait_tiles(x_hbm, buf_ref.at[1 - slot], sem_ref.at[1 - slot], MOE_ROWS)

    @pl.when(i >= n_active)
    def _():
        y_ref[...] = jnp.zeros_like(y_ref)


def _experts(xp, block_e, n_active, row_tok, row_w, w_gate, w_up, w_down, layer):
    n_blocks = row_tok.shape[0]
    n_rows = n_blocks * MOE_ROWS
    smem_blk = lambda index_map: pl.BlockSpec((1, 1, MOE_ROWS), index_map, memory_space=pltpu.SMEM)
    wspec = lambda a, b: pl.BlockSpec((1, 1, a, b), lambda i, be, na: (layer, be[i], 0, 0))

    def kernel(be_ref, na_ref, tok0_ref, tokn_ref, x_hbm, wrow_ref, wg_ref, wu_ref, wd_ref, *rest):
        _experts_kernel(be_ref, na_ref, tok0_ref, tokn_ref, x_hbm, wrow_ref, wg_ref.at[0],
                        wu_ref.at[0], wd_ref.at[0], *rest)

    grid_spec = pltpu.PrefetchScalarGridSpec(
        num_scalar_prefetch=2,
        grid=(n_blocks,),
        in_specs=[smem_blk(lambda i, be, na: (0, 0, 0)),
                  smem_blk(lambda i, be, na: (jnp.minimum(i + 1, n_blocks - 1), 0, 0)),
                  pl.BlockSpec(memory_space=pl.ANY),
                  pl.BlockSpec((MOE_ROWS, 1), lambda i, be, na: (i, 0)),
                  wspec(D_MODEL, D_EXPERT), wspec(D_MODEL, D_EXPERT), wspec(D_EXPERT, D_MODEL)],
        out_specs=pl.BlockSpec((MOE_ROWS * ROW_TILE, LANES), lambda i, be, na: (i, 0)),
        scratch_shapes=[pltpu.VMEM((2, MOE_ROWS * ROW_TILE, LANES), jnp.uint32),
                        pltpu.SemaphoreType.DMA((2,)),
                        pltpu.VMEM((D_MODEL, D_EXPERT), BF16),
                        pltpu.VMEM((D_MODEL, D_EXPERT), BF16),
                        pltpu.VMEM((D_EXPERT, D_MODEL), BF16)])
    return pl.pallas_call(
        kernel,
        out_shape=jax.ShapeDtypeStruct((n_rows * ROW_TILE, LANES), jnp.uint32),
        grid_spec=grid_spec,
        compiler_params=_cparams("arbitrary"),
        name="routed_experts",
    )(block_e, n_active, row_tok, row_tok, xp, row_w, w_gate, w_up, w_down)


def _combine_kernel(d0_ref, dn_ref, y_hbm, x_ref, wg_ref, wu_ref, wd_ref, g_ref, b_ref,
                    o_ref, o16_ref, buf_ref, sem_ref):
    i = pl.program_id(0)
    n = pl.num_programs(0)
    slot = lax.rem(i, 2)
    tm = COMBINE_TOKENS
    rows = tm * TOP_K

    @pl.when(i == 0)
    def _():
        _gather_tiles(y_hbm, d0_ref, buf_ref.at[0], sem_ref.at[0], 0, rows)

    _wait_tiles(y_hbm, buf_ref.at[slot], sem_ref.at[slot], rows)
    _gather_tiles(y_hbm, dn_ref, buf_ref.at[1 - slot], sem_ref.at[1 - slot], 0, rows)

    x = x_ref[...]
    x16 = x.astype(BF16)
    gate = jnp.dot(x16, wg_ref[...], preferred_element_type=F32)
    up = jnp.dot(x16, wu_ref[...], preferred_element_type=F32)
    h = (gate * _sigmoid(gate) * up).astype(BF16)
    v = DEEPNORM_ALPHA * x + jnp.dot(h, wd_ref[...], preferred_element_type=F32)

    lo = [None] * ROW_TILE
    hi = [None] * ROW_TILE
    for k in range(TOP_K):
        chunks = _load_token_tiles(buf_ref.at[slot], k * tm * ROW_TILE, tm)
        for s, c in enumerate(chunks):
            c_lo, c_hi = _unpack_pair(c)
            lo[s] = c_lo if k == 0 else lo[s] + c_lo
            hi[s] = c_hi if k == 0 else hi[s] + c_hi
    v = v + jnp.concatenate(lo + hi, axis=1)
    out = _layer_norm(v, g_ref[...], b_ref[...])
    o_ref[...] = out
    o16_ref[...] = out.astype(BF16)

    @pl.when(i == n - 1)
    def _():
        _wait_tiles(y_hbm, buf_ref.at[1 - slot], sem_ref.at[1 - slot], rows)


def _combine(y_rows, dest_blocks, x, w_gate_s, w_up_s, w_down_s, g, b):
    t = x.shape[0]
    tm = COMBINE_TOKENS
    rows = tm * TOP_K
    row = lambda: pl.BlockSpec((tm, D_MODEL), lambda i: (i, 0))
    const = lambda shape: pl.BlockSpec(shape, lambda i: (0, 0))
    nblk = t // tm
    smem_blk = lambda index_map: pl.BlockSpec((1, 1, rows), index_map, memory_space=pltpu.SMEM)
    return pl.pallas_call(
        _combine_kernel,
        out_shape=(jax.ShapeDtypeStruct((t, D_MODEL), F32), jax.ShapeDtypeStruct((t, D_MODEL), BF16)),
        grid=(nblk,),
        in_specs=[smem_blk(lambda i: (0, 0, 0)),
                  smem_blk(lambda i: (jnp.minimum(i + 1, nblk - 1), 0, 0)),
                  pl.BlockSpec(memory_space=pl.ANY),
                  row(),
                  const((D_MODEL, D_SHARED)), const((D_MODEL, D_SHARED)), const((D_SHARED, D_MODEL)),
                  const((1, D_MODEL)), const((1, D_MODEL))],
        out_specs=(row(), row()),
        scratch_shapes=[pltpu.VMEM((2, rows * ROW_TILE, LANES), jnp.uint32),
                        pltpu.SemaphoreType.DMA((2,))],
        compiler_params=_cparams("arbitrary"),
        name="moe_combine_layernorm",
    )(dest_blocks, dest_blocks, y_rows, x, w_gate_s, w_up_s, w_down_s, g, b)


def _dispatch_plan(idx_t, wt_t, rank_t, counts):
    t = idx_t.shape[1]
    tk = t * TOP_K
    n_blocks = tk // MOE_ROWS + N_EXPERTS
    experts = jnp.arange(N_EXPERTS, dtype=jnp.int32)
    counts = counts.reshape(N_EXPERTS).astype(jnp.int32)
    padded = (counts + MOE_ROWS - 1) // MOE_ROWS * MOE_ROWS
    pad_end = jnp.cumsum(padded)
    pad_start = pad_end - padded
    start = jnp.cumsum(counts) - counts
    first_row = jnp.sum(jnp.where(idx_t[None] == experts[:, None, None],
                                  pad_start[:, None, None], 0), axis=0)
    dest_t = (first_row + rank_t).astype(jnp.int32)
    tok = jnp.broadcast_to(jnp.arange(t, dtype=jnp.int32)[None, :], (TOP_K, t))
    _, sorted_tok, sorted_w = lax.sort((dest_t.reshape(-1), tok.reshape(-1), wt_t.reshape(-1)),
                                       num_keys=1)
    block_first = jnp.arange(n_blocks, dtype=jnp.int32) * MOE_ROWS
    block_e = jnp.minimum(jnp.sum(pad_end[None, :] <= block_first[:, None], axis=1),
                          N_EXPERTS - 1).astype(jnp.int32)
    n_active = (pad_end[-1] // MOE_ROWS).astype(jnp.int32).reshape(1)
    within = (block_first - pad_start[block_e])[:, None] + jnp.arange(MOE_ROWS, dtype=jnp.int32)[None, :]
    valid = (within < counts[block_e][:, None]) & (block_first < pad_end[-1])[:, None]
    pos = jnp.clip(start[block_e][:, None] + within, 0, tk - 1)
    row_tok = (jnp.where(valid, sorted_tok[pos], 0) * ROW_TILE).astype(jnp.int32)
    row_w = jnp.where(valid, sorted_w[pos], 0.0).astype(F32)
    return (block_e, n_active, row_tok.reshape(n_blocks, 1, MOE_ROWS),
            row_w.reshape(n_blocks * MOE_ROWS, 1), dest_t)


def _layer(x, x16, cosf, sinf, perm, expand, bsz, s, l, w_in, w_tail, w_gate_e, w_up_e, w_down_e, p):
    t = bsz * s
    zs = _matmul(x16, w_in, l, 0, D_INNER, p["zero_z"], "silu", BF16, 2048, 1024)
    xbc = _matmul(x16, w_in, l, D_INNER, CONV_DIM, p["zero_xbc"], "none", BF16, 2048, 1024)
    dt = _matmul(x16, w_in, l, DT_COL, HEAD_PAD, p["dt_bias"], "softplus", F32, 1024, HEAD_PAD)
    qkv = _matmul(x16, w_tail, l, 0, 3 * ATT_WIDTH, p["zero_qkv"], "none", BF16, 2048, 768)
    gates = _matmul(x16, w_tail, l, 3 * ATT_WIDTH, 2 * D_MODEL, p["gate_b"], "sigmoid", BF16, 2048, 512)

    y_ssd = _ssd(xbc.reshape(bsz, s, CONV_DIM), zs.reshape(bsz, s, D_INNER),
                 dt.reshape(bsz, s, HEAD_PAD), p["conv_w"], p["conv_b"], p["a_log"], p["d_skip"],
                 p["norm_w"], expand).reshape(t, D_INNER)

    qkv3 = qkv.reshape(bsz, s, 3 * ATT_WIDTH)
    outs, lses = [], []
    for gi, (_, dilation) in enumerate(ATT_GROUPS):
        o, lse = _attention_group(qkv3, cosf, sinf, perm, gi, dilation)
        outs.append(o)
        lses.append(lse)

    merged = _merge(y_ssd, outs, lses, p["w_ssd_br"], p["w_att_br"], gates, s)
    x1, x1p = _proj_ln(merged, p["w_o"], x, p["ln1_g"], p["ln1_b"])

    idx_t, wt_t, rank_t, counts = _router(x1, p["w_router_t"], p["router_bias"])
    block_e, n_active, row_tok, row_w, dest_t = _dispatch_plan(idx_t, wt_t, rank_t, counts)
    y_rows = _experts(x1p, block_e, n_active, row_tok, row_w, w_gate_e, w_up_e, w_down_e, l)
    nt = t // COMBINE_TOKENS
    dest_blocks = (dest_t * ROW_TILE).reshape(TOP_K, nt, COMBINE_TOKENS).transpose(1, 0, 2)
    dest_blocks = dest_blocks.reshape(nt, 1, TOP_K * COMBINE_TOKENS)
    return _combine(y_rows, dest_blocks, x1, p["w_gate_s"], p["w_up_s"], p["w_down_s"],
                    p["ln2_g"], p["ln2_b"])


def _pad_heads(v):
    return jnp.pad(v.astype(F32), (0, HEAD_PAD - SSD_HEADS)).reshape(1, HEAD_PAD)


def _layer_params(l, conv_w, conv_b, dt_bias, a_log, d_skip, ssd_norm_w, w_ssd_br, w_att_br,
                  gate_b, w_o, ln1_g, ln1_b, w_router, router_bias, w_gate_s, w_up_s, w_down_s,
                  ln2_g, ln2_b):
    row = lambda v: v.astype(F32).reshape(1, -1)
    return dict(
        zero_z=jnp.zeros((1, D_INNER), F32), zero_xbc=jnp.zeros((1, CONV_DIM), F32),
        zero_qkv=jnp.zeros((1, 3 * ATT_WIDTH), F32),
        dt_bias=_pad_heads(dt_bias[l]), gate_b=row(gate_b[l]),
        conv_w=conv_w[l].astype(F32), conv_b=row(conv_b[l]),
        a_log=_pad_heads(a_log[l]), d_skip=_pad_heads(d_skip[l]), norm_w=row(ssd_norm_w[l]),
        w_ssd_br=w_ssd_br[l].astype(BF16), w_att_br=w_att_br[l].astype(BF16),
        w_o=w_o[l].astype(BF16), ln1_g=row(ln1_g[l]), ln1_b=row(ln1_b[l]),
        w_router_t=w_router[l].astype(F32).T, router_bias=router_bias[l].astype(F32).reshape(-1, 1),
        w_gate_s=w_gate_s[l].astype(BF16), w_up_s=w_up_s[l].astype(BF16),
        w_down_s=w_down_s[l].astype(BF16), ln2_g=row(ln2_g[l]), ln2_b=row(ln2_b[l]))


def _rope_tables(positions):
    half = ROPE_DIM // 2
    inv_freq = ROPE_THETA ** (-jnp.arange(0, ROPE_DIM, 2, dtype=F32) / ROPE_DIM)
    ang = positions.astype(F32)[..., None] * inv_freq
    cos, sin = jnp.cos(ang), jnp.sin(ang)
    pad = positions.shape + (ATT_HEAD_DIM - ROPE_DIM,)
    cosf = jnp.concatenate([cos, cos, jnp.ones(pad, F32)], axis=-1)
    sinf = jnp.concatenate([-sin, sin, jnp.zeros(pad, F32)], axis=-1)
    k = jnp.arange(ATT_HEAD_DIM)[:, None]
    j = jnp.arange(ATT_HEAD_DIM)[None, :]
    perm = ((j < half) & (k == j + half)) | ((j >= half) & (j < ROPE_DIM) & (k == j - half))
    return cosf, sinf, perm.astype(BF16)


def kernel(x, positions, w_in, conv_w, conv_b, dt_bias, a_log, d_skip, ssd_norm_w, w_ssd_br,
           w_att_br, gate_b, w_o, ln1_g, ln1_b, w_router, router_bias, w_gate_e, w_up_e, w_down_e,
           w_gate_s, w_up_s, w_down_s, ln2_g, ln2_b):
    bsz, s, d = x.shape
    cosf, sinf, perm = _rope_tables(positions)
    head_of_col = jnp.arange(D_INNER) // SSD_HEAD_DIM
    expand = (jnp.arange(HEAD_PAD)[:, None] == head_of_col[None, :]).astype(BF16)
    w_in = w_in.astype(F32)
    w_tail = w_in[:, :, TAIL_COL:].astype(BF16)
    w_gate_e, w_up_e, w_down_e = (w.astype(F32) for w in (w_gate_e, w_up_e, w_down_e))
    xf = x.reshape(bsz * s, d).astype(F32)
    x16 = xf.astype(BF16)
    for l in range(DEPTH):
        p = _layer_params(l, conv_w, conv_b, dt_bias, a_log, d_skip, ssd_norm_w, w_ssd_br, w_att_br,
                          gate_b, w_o, ln1_g, ln1_b, w_router, router_bias, w_gate_s, w_up_s,
                          w_down_s, ln2_g, ln2_b)
        xf, x16 = _layer(xf, x16, cosf, sinf, perm, expand, bsz, s, l, w_in, w_tail,
                         w_gate_e, w_up_e, w_down_e, p)
    return xf.reshape(bsz, s, d).astype(x.dtype)
```
